```python
import jax, jax.numpy as jnp
from jax import lax
import numpy as np

D_MODEL = 1024
BATCH = 16
SEQ = 2048
DEPTH = 4

GRID_W = 64
MIX_WIDTH = 2 * D_MODEL
FOURIER_WIDTH = MIX_WIDTH // 2
N_FOURIER_GROUPS = 4
FOURIER_GROUP = FOURIER_WIDTH // N_FOURIER_GROUPS
ATTN_WIDTH = MIX_WIDTH - FOURIER_WIDTH
HEAD_DIM = 128
N_Q_HEADS = ATTN_WIDTH // HEAD_DIM
N_KV_HEADS = 2
Q_PER_KV = N_Q_HEADS // N_KV_HEADS
KV_WIDTH = N_KV_HEADS * HEAD_DIM
Q_BLOCK = 128
ROPE_THETA = 10000.0
ROPE_AXIS_DIM = HEAD_DIM // 2
RNN_WIDTH = MIX_WIDTH
N_RNN_BLOCKS = 16
RNN_BLOCK = RNN_WIDTH // N_RNN_BLOCKS
CONV_WIDTH = 4
CONV_LEFT = 2
RG_LRU_C = 8.0
NORM_EPS = 1e-6
EVEN_IN_WIDTH = 2 * FOURIER_WIDTH + ATTN_WIDTH + 2 * KV_WIDTH + ATTN_WIDTH
EVEN_SPLITS = (FOURIER_WIDTH,
               2 * FOURIER_WIDTH,
               2 * FOURIER_WIDTH + ATTN_WIDTH,
               2 * FOURIER_WIDTH + ATTN_WIDTH + KV_WIDTH,
               2 * FOURIER_WIDTH + ATTN_WIDTH + 2 * KV_WIDTH)

kernel_name = "fourier_gqa_rglru_hybrid_encoder"


def rms_norm(x, g):
    xf = x.astype(jnp.float32)
    y = xf * lax.rsqrt(jnp.mean(xf * xf, axis=-1, keepdims=True) + NORM_EPS)
    return (y * g.astype(jnp.float32)).astype(x.dtype)


def axial_rope_tables(seq_len):
    rows = seq_len // GRID_W
    row = jnp.broadcast_to(jnp.arange(rows)[:, None], (rows, GRID_W)).reshape(-1).astype(jnp.float32)
    col = jnp.broadcast_to(jnp.arange(GRID_W)[None, :], (rows, GRID_W)).reshape(-1).astype(jnp.float32)
    inv_freq = ROPE_THETA ** (-jnp.arange(0, ROPE_AXIS_DIM, 2, dtype=jnp.float32) / ROPE_AXIS_DIM)
    ang_r = row[:, None] * inv_freq[None, :]
    ang_c = col[:, None] * inv_freq[None, :]
    return (jnp.cos(ang_r), jnp.sin(ang_r), jnp.cos(ang_c), jnp.sin(ang_c))


def rope_half(x, cos, sin):
    x1, x2 = jnp.split(x, 2, axis=-1)
    c = cos[:, None, :]
    s = sin[:, None, :]
    return jnp.concatenate([x1 * c - x2 * s, x1 * s + x2 * c], axis=-1)


def axial_rope(x, tables):
    cos_r, sin_r, cos_c, sin_c = tables
    xf = x.astype(jnp.float32)
    x_row, x_col = jnp.split(xf, 2, axis=-1)
    out = jnp.concatenate([rope_half(x_row, cos_r, sin_r), rope_half(x_col, cos_c, sin_c)], axis=-1)
    return out.astype(x.dtype)


def blocked_attention(q, k, v):
    b, s = q.shape[0], q.shape[1]
    n_blocks = s // Q_BLOCK
    qb = q.reshape(b, n_blocks, Q_BLOCK, N_KV_HEADS, Q_PER_KV, HEAD_DIM).transpose(1, 0, 2, 3, 4, 5)
    scale = HEAD_DIM ** -0.5

    def one_block(q_blk):
        scores = jnp.einsum('bqhgd,bkhd->bhgqk', q_blk, k,
                            preferred_element_type=jnp.float32) * scale
        probs = jax.nn.softmax(scores, axis=-1).astype(v.dtype)
        return jnp.einsum('bhgqk,bkhd->bqhgd', probs, v)

    out = lax.map(one_block, qb)
    return out.transpose(1, 0, 2, 3, 4, 5).reshape(b, s, N_Q_HEADS * HEAD_DIM)


def fourier_attention_layer(x, norm_g, w_in, fourier_w, q_gain, k_gain, w_out, rope):
    b, s, _ = x.shape
    h = rms_norm(x, norm_g)
    proj = h @ w_in
    f_in, f_gate, q, k, v, a_gate = jnp.split(proj, EVEN_SPLITS, axis=-1)
    fg = f_in.astype(jnp.float32).reshape(b, s, N_FOURIER_GROUPS, FOURIER_GROUP)
    fmix = jnp.fft.fftn(fg, axes=(1, 3), norm="ortho").real.astype(x.dtype)
    fmix = jnp.einsum('bsgc,gce->bsge', fmix, fourier_w).reshape(b, s, FOURIER_WIDTH)
    f_out = fmix * jax.nn.silu(f_gate)
    q = rms_norm(q.reshape(b, s, N_Q_HEADS, HEAD_DIM), q_gain)
    k = rms_norm(k.reshape(b, s, N_KV_HEADS, HEAD_DIM), k_gain)
    v = v.reshape(b, s, N_KV_HEADS, HEAD_DIM)
    q = axial_rope(q, rope)
    k = axial_rope(k, rope)
    a_out = blocked_attention(q, k, v) * jax.nn.silu(a_gate)
    return x + jnp.concatenate([f_out, a_out], axis=-1) @ w_out


def rglru_scan(xc, w_a, b_a, w_x, b_x, lam, reverse):
    b, s, _ = xc.shape
    xb = xc.astype(jnp.float32).reshape(b, s, N_RNN_BLOCKS, RNN_BLOCK)
    r = jax.nn.sigmoid(jnp.einsum('bshc,hce->bshe', xb, w_a.astype(jnp.float32)) + b_a).reshape(b, s, RNN_WIDTH)
    i = jax.nn.sigmoid(jnp.einsum('bshc,hce->bshe', xb, w_x.astype(jnp.float32)) + b_x).reshape(b, s, RNN_WIDTH)
    log_a = -RG_LRU_C * r * jax.nn.softplus(-lam.astype(jnp.float32))
    a = jnp.exp(log_a)
    u = jnp.sqrt(-jnp.expm1(2.0 * log_a)) * (i * xb.reshape(b, s, RNN_WIDTH))

    def combine(left, right):
        a1, h1 = left
        a2, h2 = right
        return a1 * a2, a2 * h1 + h2

    _, hseq = lax.associative_scan(combine, (a, u), reverse=reverse, axis=1)
    return hseq


def rglru_layer(x, norm_g, w_in, conv_w, conv_b, w_a, b_a, w_x, b_x, lam, w_out):
    s = x.shape[1]
    h = rms_norm(x, norm_g)
    xr, gate = jnp.split(h @ w_in, 2, axis=-1)
    xp = jnp.pad(xr, ((0, 0), (CONV_LEFT, CONV_WIDTH - 1 - CONV_LEFT), (0, 0)))
    xc = conv_b + xp[:, 0:s] * conv_w[0]
    for j in range(1, CONV_WIDTH):
        xc = xc + xp[:, j:j + s] * conv_w[j]
    y = (rglru_scan(xc, w_a[0], b_a[0], w_x[0], b_x[0], lam[0], reverse=False)
         + rglru_scan(xc, w_a[1], b_a[1], w_x[1], b_x[1], lam[1], reverse=True))
    return x + (y.astype(x.dtype) * jax.nn.silu(gate)) @ w_out


def setup_inputs(seed: int = 0) -> dict:
    key = jax.random.key(seed)
    ks = jax.random.split(key, 20)
    n_even = (DEPTH + 1) // 2
    n_odd = DEPTH // 2

    def nrm(k, shape, scale):
        return jax.random.normal(k, shape, jnp.float32) * scale

    x = nrm(ks[0], (BATCH, SEQ, D_MODEL), 1.0)
    even_norm = 1.0 + nrm(ks[1], (n_even, D_MODEL), 0.02)
    even_w_in = nrm(ks[2], (n_even, D_MODEL, EVEN_IN_WIDTH), D_MODEL ** -0.5)
    fourier_w = nrm(ks[3], (n_even, N_FOURIER_GROUPS, FOURIER_GROUP, FOURIER_GROUP), FOURIER_GROUP ** -0.5)
    q_gain = 1.0 + nrm(ks[4], (n_even, HEAD_DIM), 0.02)
    k_gain = 1.0 + nrm(ks[5], (n_even, HEAD_DIM), 0.02)
    even_w_out = nrm(ks[6], (n_even, MIX_WIDTH, D_MODEL), MIX_WIDTH ** -0.5)
    odd_norm = 1.0 + nrm(ks[7], (n_odd, D_MODEL), 0.02)
    odd_w_in = nrm(ks[8], (n_odd, D_MODEL, 2 * RNN_WIDTH), D_MODEL ** -0.5)
    conv_w = nrm(ks[9], (n_odd, CONV_WIDTH, RNN_WIDTH), CONV_WIDTH ** -0.5)
    conv_b = nrm(ks[10], (n_odd, RNN_WIDTH), 0.01)
    gate_a_w = nrm(ks[11], (n_odd, 2, N_RNN_BLOCKS, RNN_BLOCK, RNN_BLOCK), RNN_BLOCK ** -0.5)
    gate_a_b = nrm(ks[12], (n_odd, 2, N_RNN_BLOCKS, RNN_BLOCK), 0.01)
    gate_x_w = nrm(ks[13], (n_odd, 2, N_RNN_BLOCKS, RNN_BLOCK, RNN_BLOCK), RNN_BLOCK ** -0.5)
    gate_x_b = nrm(ks[14], (n_odd, 2, N_RNN_BLOCKS, RNN_BLOCK), 0.01)
    a_pow = jax.random.uniform(ks[15], (n_odd, 2, RNN_WIDTH), jnp.float32, minval=0.9, maxval=0.999)
    a0 = a_pow ** (1.0 / RG_LRU_C)
    rglru_lambda = jnp.log(a0) - jnp.log1p(-a0)
    odd_w_out = nrm(ks[16], (n_odd, RNN_WIDTH, D_MODEL), RNN_WIDTH ** -0.5)
    final_norm = 1.0 + nrm(ks[17], (D_MODEL,), 0.02)
    return {"x": x, "even_norm": even_norm, "even_w_in": even_w_in, "fourier_w": fourier_w,
            "q_gain": q_gain, "k_gain": k_gain, "even_w_out": even_w_out,
            "odd_norm": odd_norm, "odd_w_in": odd_w_in, "conv_w": conv_w, "conv_b": conv_b,
            "gate_a_w": gate_a_w, "gate_a_b": gate_a_b, "gate_x_w": gate_x_w, "gate_x_b": gate_x_b,
            "rglru_lambda": rglru_lambda, "odd_w_out": odd_w_out, "final_norm": final_norm}


def reference(x, even_norm, even_w_in, fourier_w, q_gain, k_gain, even_w_out,
              odd_norm, odd_w_in, conv_w, conv_b, gate_a_w, gate_a_b, gate_x_w, gate_x_b,
              rglru_lambda, odd_w_out, final_norm):
    rope = axial_rope_tables(x.shape[1])
    h = x
    for layer in range(DEPTH):
        i = layer // 2
        if layer % 2 == 0:
            h = fourier_attention_layer(h, even_norm[i], even_w_in[i], fourier_w[i],
                                        q_gain[i], k_gain[i], even_w_out[i], rope)
        else:
            h = rglru_layer(h, odd_norm[i], odd_w_in[i], conv_w[i], conv_b[i],
                            gate_a_w[i], gate_a_b[i], gate_x_w[i], gate_x_b[i],
                            rglru_lambda[i], odd_w_out[i])
    return rms_norm(h, final_norm)
```

```python
import functools

import jax
import jax.numpy as jnp
import numpy as np
from jax import lax
from jax.experimental import pallas as pl
from jax.experimental.pallas import tpu as pltpu

D_MODEL = 1024
BATCH = 16
SEQ = 2048
DEPTH = 4
TOKENS = BATCH * SEQ

GRID_W = 64
MIX_WIDTH = 2 * D_MODEL
FOURIER_WIDTH = MIX_WIDTH // 2
N_FOURIER_GROUPS = 4
FOURIER_GROUP = FOURIER_WIDTH // N_FOURIER_GROUPS
ATTN_WIDTH = MIX_WIDTH - FOURIER_WIDTH
HEAD_DIM = 128
N_Q_HEADS = ATTN_WIDTH // HEAD_DIM
N_KV_HEADS = 2
Q_PER_KV = N_Q_HEADS // N_KV_HEADS
KV_WIDTH = N_KV_HEADS * HEAD_DIM
ROPE_THETA = 10000.0
ROPE_AXIS_DIM = HEAD_DIM // 2
RNN_WIDTH = MIX_WIDTH
N_RNN_BLOCKS = 16
RNN_BLOCK = RNN_WIDTH // N_RNN_BLOCKS
CONV_WIDTH = 4
CONV_LEFT = 2
RG_LRU_C = 8.0
NORM_EPS = 1e-6
EVEN_IN_WIDTH = 2 * FOURIER_WIDTH + ATTN_WIDTH + 2 * KV_WIDTH + ATTN_WIDTH

F32 = jnp.float32
BF16 = jnp.bfloat16

VMEM_LIMIT_BYTES = 56 * 1024 * 1024

ROW_TILE = 512
ATTN_Q_TILE = 256
SCAN_CHUNK_STEPS = 128
SCAN_CHUNK_ROWS = SCAN_CHUNK_STEPS * BATCH
N_SCAN_CHUNKS = SEQ // SCAN_CHUNK_STEPS
HALO_PREV_ROWS = CONV_LEFT * BATCH
HALO_NEXT_ROWS = (CONV_WIDTH - 1 - CONV_LEFT) * BATCH


def _resident(block_shape, index_map):
    return pl.BlockSpec(block_shape, index_map, pipeline_mode=pl.Buffered(1))


def _params(n_axes):
    return pltpu.CompilerParams(
        dimension_semantics=("arbitrary",) * n_axes,
        vmem_limit_bytes=VMEM_LIMIT_BYTES)


def _rms_norm(x, gain):
    return x * lax.rsqrt(jnp.mean(x * x, axis=-1, keepdims=True) + NORM_EPS) * gain


def _silu(x):
    return x * jax.nn.sigmoid(x)


def _even_in_kernel(x_ref, g_ref, w_ref, qg_ref, kg_ref, cos_ref, sina_ref, sinb_ref,
                    cdft_ref, xcs_ref, fgate_ref, q_ref, k_ref, v_ref, agate_ref):
    hb = _rms_norm(x_ref[...], g_ref[...]).astype(BF16)

    def proj(lo, width):
        return jnp.dot(hb, w_ref[:, lo:lo + width], preferred_element_type=F32)

    f_in = proj(0, FOURIER_WIDTH).astype(BF16)
    for g in range(N_FOURIER_GROUPS):
        lo = g * FOURIER_GROUP
        z = jnp.dot(f_in[:, lo:lo + FOURIER_GROUP], cdft_ref[...], preferred_element_type=F32)
        xcs_ref[0, :, lo:lo + FOURIER_GROUP] = z[:, :FOURIER_GROUP].astype(BF16)
        xcs_ref[1, :, lo:lo + FOURIER_GROUP] = z[:, FOURIER_GROUP:].astype(BF16)
    fgate_ref[...] = _silu(proj(FOURIER_WIDTH, FOURIER_WIDTH))

    cos = cos_ref[...]
    sina = sina_ref[...]
    sinb = sinb_ref[...]

    def norm_rope(y, gain):
        yn = _rms_norm(y, gain)
        return (yn * cos + pltpu.roll(yn, HEAD_DIM - ROPE_AXIS_DIM // 2, 1) * sina
                + pltpu.roll(yn, ROPE_AXIS_DIM // 2, 1) * sinb)

    q_lo = 2 * FOURIER_WIDTH
    q = proj(q_lo, ATTN_WIDTH)
    for h in range(N_Q_HEADS):
        lo = h * HEAD_DIM
        q_ref[:, lo:lo + HEAD_DIM] = norm_rope(q[:, lo:lo + HEAD_DIM], qg_ref[...]).astype(BF16)
    kv = proj(q_lo + ATTN_WIDTH, 2 * KV_WIDTH)
    for h in range(N_KV_HEADS):
        lo = h * HEAD_DIM
        k_ref[:, lo:lo + HEAD_DIM] = norm_rope(kv[:, lo:lo + HEAD_DIM], kg_ref[...]).astype(BF16)
    v_ref[...] = kv[:, KV_WIDTH:].astype(BF16)
    agate_ref[...] = _silu(proj(q_lo + ATTN_WIDTH + 2 * KV_WIDTH, ATTN_WIDTH))


def _even_in(x, norm_g, w_in, q_gain, k_gain, rope, cdft):
    tm = ROW_TILE
    row = lambda i: (i, 0)
    fixed = lambda i: (0, 0)
    return pl.pallas_call(
        _even_in_kernel,
        grid=(TOKENS // tm,),
        in_specs=[
            pl.BlockSpec((tm, D_MODEL), row),
            _resident((1, D_MODEL), fixed),
            _resident((D_MODEL, EVEN_IN_WIDTH), fixed),
            _resident((1, HEAD_DIM), fixed),
            _resident((1, HEAD_DIM), fixed),
            pl.BlockSpec((tm, HEAD_DIM), row),
            pl.BlockSpec((tm, HEAD_DIM), row),
            pl.BlockSpec((tm, HEAD_DIM), row),
            _resident((FOURIER_GROUP, 2 * FOURIER_GROUP), fixed),
        ],
        out_specs=[
            pl.BlockSpec((2, tm, FOURIER_WIDTH), lambda i: (0, i, 0)),
            pl.BlockSpec((tm, FOURIER_WIDTH), row),
            pl.BlockSpec((tm, ATTN_WIDTH), row),
            pl.BlockSpec((tm, KV_WIDTH), row),
            pl.BlockSpec((tm, KV_WIDTH), row),
            pl.BlockSpec((tm, ATTN_WIDTH), row),
        ],
        out_shape=[
            jax.ShapeDtypeStruct((2, TOKENS, FOURIER_WIDTH), BF16),
            jax.ShapeDtypeStruct((TOKENS, FOURIER_WIDTH), F32),
            jax.ShapeDtypeStruct((TOKENS, ATTN_WIDTH), BF16),
            jax.ShapeDtypeStruct((TOKENS, KV_WIDTH), BF16),
            jax.ShapeDtypeStruct((TOKENS, KV_WIDTH), BF16),
            jax.ShapeDtypeStruct((TOKENS, ATTN_WIDTH), F32),
        ],
        compiler_params=_params(1),
        name="even_in_proj",
    )(x, norm_g, w_in, q_gain, k_gain, *rope, cdft)


def _fourier_kernel(xcs_ref, dft_ref, fw_ref, gate_ref, o_ref):
    rhs = xcs_ref[...].reshape(2 * SEQ, FOURIER_GROUP)
    y = jnp.dot(dft_ref[...], rhs, preferred_element_type=F32)
    y = y * np.float32(1.0 / np.sqrt(SEQ * FOURIER_GROUP))
    fm = jnp.dot(y.astype(BF16), fw_ref[0], preferred_element_type=F32)
    o_ref[...] = (fm * gate_ref[...]).astype(BF16)


def _fourier_mix(xcs, seq_dft, fourier_w, fgate):
    n_blocks = BATCH * N_FOURIER_GROUPS
    return pl.pallas_call(
        _fourier_kernel,
        grid=(n_blocks,),
        in_specs=[
            pl.BlockSpec((2, SEQ, FOURIER_GROUP), lambda n: (0, 0, n)),
            _resident((SEQ, 2 * SEQ), lambda n: (0, 0)),
            pl.BlockSpec((1, FOURIER_GROUP, FOURIER_GROUP), lambda n: (n % N_FOURIER_GROUPS, 0, 0)),
            pl.BlockSpec((SEQ, FOURIER_GROUP), lambda n: (0, n)),
        ],
        out_specs=pl.BlockSpec((SEQ, FOURIER_GROUP), lambda n: (0, n)),
        out_shape=jax.ShapeDtypeStruct((SEQ, BATCH * FOURIER_WIDTH), BF16),
        compiler_params=_params(1),
        name="fourier_mix",
    )(xcs.reshape(2, SEQ, BATCH * FOURIER_WIDTH), seq_dft, fourier_w,
      fgate.reshape(SEQ, BATCH * FOURIER_WIDTH)).reshape(TOKENS, FOURIER_WIDTH)


def _attention_kernel(q_ref, k_ref, v_ref, gate_ref, o_ref):
    k = k_ref[...]
    v = v_ref[...]
    scale = np.float32(HEAD_DIM ** -0.5)

    def q_tile(i, carry):
        r = pl.multiple_of(i * ATTN_Q_TILE, ATTN_Q_TILE)
        rows = pl.ds(r, ATTN_Q_TILE)
        for h in range(Q_PER_KV):
            cols = slice(h * HEAD_DIM, (h + 1) * HEAD_DIM)
            s = lax.dot_general(q_ref[rows, cols], k, (((1,), (1,)), ((), ())),
                                preferred_element_type=F32) * scale
            p = jnp.exp(s - jnp.max(s, axis=-1, keepdims=True))
            probs = p / jnp.sum(p, axis=-1, keepdims=True)
            o = jnp.dot(probs.astype(BF16), v, preferred_element_type=F32)
            o_ref[rows, cols] = (o * gate_ref[rows, cols]).astype(BF16)
        return carry

    lax.fori_loop(0, SEQ // ATTN_Q_TILE, q_tile, 0)


def _attention(q, k, v, agate):
    group_width = Q_PER_KV * HEAD_DIM
    blk = lambda b, h: (0, b * N_KV_HEADS + h)
    return pl.pallas_call(
        _attention_kernel,
        grid=(BATCH, N_KV_HEADS),
        in_specs=[
            pl.BlockSpec((SEQ, group_width), blk),
            pl.BlockSpec((SEQ, HEAD_DIM), blk),
            pl.BlockSpec((SEQ, HEAD_DIM), blk),
            pl.BlockSpec((SEQ, group_width), blk),
        ],
        out_specs=pl.BlockSpec((SEQ, group_width), blk),
        out_shape=jax.ShapeDtypeStruct((SEQ, BATCH * ATTN_WIDTH), BF16),
        compiler_params=_params(2),
        name="gqa_attention",
    )(q.reshape(SEQ, BATCH * ATTN_WIDTH), k.reshape(SEQ, BATCH * KV_WIDTH),
      v.reshape(SEQ, BATCH * KV_WIDTH),
      agate.reshape(SEQ, BATCH * ATTN_WIDTH)).reshape(TOKENS, ATTN_WIDTH)


def _out_proj_kernel(*refs, n_mix, final_norm):
    x_ref = refs[0]
    mix_refs = refs[1:1 + n_mix]
    w_ref = refs[1 + n_mix]
    o_ref = refs[-1]
    acc = x_ref[...]
    lo = 0
    for m_ref in mix_refs:
        width = m_ref.shape[1]
        acc = acc + jnp.dot(m_ref[...], w_ref[lo:lo + width, :], preferred_element_type=F32)
        lo += width
    if final_norm:
        acc = _rms_norm(acc, refs[2 + n_mix][...])
    o_ref[...] = acc


def _out_proj(x, mixes, w_out, final_g=None):
    tm = ROW_TILE
    row = lambda i: (i, 0)
    fixed = lambda i: (0, 0)
    in_specs = [pl.BlockSpec((tm, D_MODEL), row)]
    in_specs += [pl.BlockSpec((tm, m.shape[1]), row) for m in mixes]
    in_specs += [_resident((MIX_WIDTH, D_MODEL), fixed)]
    args = [x, *mixes, w_out]
    if final_g is not None:
        in_specs += [_resident((1, D_MODEL), fixed)]
        args += [final_g]
    return pl.pallas_call(
        functools.partial(_out_proj_kernel, n_mix=len(mixes), final_norm=final_g is not None),
        grid=(TOKENS // tm,),
        in_specs=in_specs,
        out_specs=pl.BlockSpec((tm, D_MODEL), row),
        out_shape=jax.ShapeDtypeStruct((TOKENS, D_MODEL), F32),
        compiler_params=_params(1),
        name="out_proj",
    )(*args)


def _odd_in_kernel(x_ref, g_ref, w_ref, xr_ref, gate_ref):
    hb = _rms_norm(x_ref[...], g_ref[...]).astype(BF16)
    xr_ref[...] = jnp.dot(hb, w_ref[:, :RNN_WIDTH], preferred_element_type=F32)
    gate_ref[...] = _silu(jnp.dot(hb, w_ref[:, RNN_WIDTH:], preferred_element_type=F32))


def _odd_in(x, norm_g, w_in):
    tm = ROW_TILE
    row = lambda i: (i, 0)
    fixed = lambda i: (0, 0)
    return pl.pallas_call(
        _odd_in_kernel,
        grid=(TOKENS // tm,),
        in_specs=[
            pl.BlockSpec((tm, D_MODEL), row),
            _resident((1, D_MODEL), fixed),
            _resident((D_MODEL, 2 * RNN_WIDTH), fixed),
        ],
        out_specs=[pl.BlockSpec((tm, RNN_WIDTH), row), pl.BlockSpec((tm, RNN_WIDTH), row)],
        out_shape=[jax.ShapeDtypeStruct((TOKENS, RNN_WIDTH), F32)] * 2,
        compiler_params=_params(1),
        name="odd_in_proj",
    )(x, norm_g, w_in)


def _gate_terms(z, xc, log_decay):
    r_gate = 0.5 * jnp.tanh(0.5 * z[:, :RNN_BLOCK]) + 0.5
    i_gate = 0.5 * jnp.tanh(0.5 * z[:, RNN_BLOCK:]) + 0.5
    log_a = log_decay * r_gate
    a = jnp.exp(log_a)
    u = jnp.sqrt(-jnp.tanh(log_a) * (a * a + 1.0)) * (i_gate * xc)
    return a, u


def _rglru_kernel(xr_ref, prev_ref, next_ref, cw_ref, cb_ref, gw_ref, gb_ref, lam_ref,
                  gate_ref, y_ref, ext_ref, xc_all_ref, hf_all_ref, z_ref, h_ref):
    phase = pl.program_id(1)
    j = pl.program_id(2)
    n_rows = SCAN_CHUNK_ROWS

    @pl.when(j == 0)
    def _():
        h_ref[...] = jnp.zeros_like(h_ref)

    log_decay = jnp.broadcast_to(-RG_LRU_C * jax.nn.softplus(-lam_ref[0]), (BATCH, RNN_BLOCK))

    def gates(base):
        xc = xc_all_ref[pl.ds(base, n_rows), :]
        z_ref[...] = jnp.dot(xc.astype(BF16), gw_ref[0, 0], preferred_element_type=F32) + gb_ref[0]

    def scan_step(base, t, h):
        rows = pl.ds(pl.multiple_of(t * BATCH, BATCH), BATCH)
        rows_all = pl.ds(pl.multiple_of(base + t * BATCH, BATCH), BATCH)
        a, u = _gate_terms(z_ref[rows, :], xc_all_ref[rows_all, :], log_decay)
        return rows, rows_all, a * h + u

    @pl.when(phase == 0)
    def _():
        base = pl.multiple_of(j * n_rows, n_rows)
        ext_ref[:HALO_PREV_ROWS, :] = jnp.where(j > 0, prev_ref[...], 0.0)
        ext_ref[HALO_PREV_ROWS:HALO_PREV_ROWS + n_rows, :] = xr_ref[...]
        ext_ref[HALO_PREV_ROWS + n_rows:, :] = jnp.where(j < N_SCAN_CHUNKS - 1, next_ref[...], 0.0)
        xc = cb_ref[...] + ext_ref[0:n_rows, :] * cw_ref[0:1, :]
        for tap in range(1, CONV_WIDTH):
            xc = xc + ext_ref[tap * BATCH:tap * BATCH + n_rows, :] * cw_ref[tap:tap + 1, :]
        xc_all_ref[pl.ds(base, n_rows), :] = xc
        gates(base)

        def step(t, h):
            _, rows_all, h = scan_step(base, t, h)
            hf_all_ref[rows_all, :] = h
            return h

        h_ref[...] = lax.fori_loop(0, SCAN_CHUNK_STEPS, step, h_ref[...], unroll=4)

    @pl.when(phase == 1)
    def _():
        base = pl.multiple_of((N_SCAN_CHUNKS - 1 - j) * n_rows, n_rows)
        gates(base)

        def step(i, h):
            rows, rows_all, h = scan_step(base, SCAN_CHUNK_STEPS - 1 - i, h)
            y = (hf_all_ref[rows_all, :] + h) * gate_ref[rows, :]
            y_ref[rows, :] = y.astype(BF16)
            return h

        h_ref[...] = lax.fori_loop(0, SCAN_CHUNK_STEPS, step, h_ref[...], unroll=4)


def _rglru_mix(xr, gate, conv_w, conv_b, gate_w, gate_b, lam):
    n = N_SCAN_CHUNKS
    rows = SCAN_CHUNK_ROWS
    prev_per_chunk = rows // HALO_PREV_ROWS
    next_per_chunk = rows // HALO_NEXT_ROWS
    last_next = TOKENS // HALO_NEXT_ROWS - 1

    def fwd_chunk(p, j):
        return j * (1 - p) + (n - 1) * p

    def bwd_chunk(p, j):
        return (n - 1) - j * p

    return pl.pallas_call(
        _rglru_kernel,
        grid=(N_RNN_BLOCKS, 2, n),
        in_specs=[
            pl.BlockSpec((rows, RNN_BLOCK), lambda c, p, j: (fwd_chunk(p, j), c)),
            pl.BlockSpec((HALO_PREV_ROWS, RNN_BLOCK),
                         lambda c, p, j: (jnp.maximum(fwd_chunk(p, j) * prev_per_chunk - 1, 0), c)),
            pl.BlockSpec((HALO_NEXT_ROWS, RNN_BLOCK),
                         lambda c, p, j: (jnp.minimum((fwd_chunk(p, j) + 1) * next_per_chunk, last_next), c)),
            pl.BlockSpec((CONV_WIDTH, RNN_BLOCK), lambda c, p, j: (0, c)),
            pl.BlockSpec((1, RNN_BLOCK), lambda c, p, j: (0, c)),
            pl.BlockSpec((1, 1, RNN_BLOCK, 2 * RNN_BLOCK), lambda c, p, j: (p, c, 0, 0)),
            pl.BlockSpec((1, 1, 2 * RNN_BLOCK), lambda c, p, j: (p * N_RNN_BLOCKS + c, 0, 0)),
            pl.BlockSpec((1, 1, RNN_BLOCK), lambda c, p, j: (p * N_RNN_BLOCKS + c, 0, 0)),
            pl.BlockSpec((rows, RNN_BLOCK), lambda c, p, j: (bwd_chunk(p, j), c)),
        ],
        out_specs=pl.BlockSpec((rows, RNN_BLOCK), lambda c, p, j: (bwd_chunk(p, j), c)),
        out_shape=jax.ShapeDtypeStruct((TOKENS, RNN_WIDTH), BF16),
        scratch_shapes=[
            pltpu.VMEM((rows + HALO_PREV_ROWS + HALO_NEXT_ROWS, RNN_BLOCK), F32),
            pltpu.VMEM((TOKENS, RNN_BLOCK), F32),
            pltpu.VMEM((TOKENS, RNN_BLOCK), F32),
            pltpu.VMEM((rows, 2 * RNN_BLOCK), F32),
            pltpu.VMEM((BATCH, RNN_BLOCK), F32),
        ],
        compiler_params=_params(3),
        name="rglru_mix",
    )(xr, xr, xr, conv_w, conv_b, gate_w, gate_b, lam, gate)


def _dft_matrix(n, sin_sign):
    j = np.arange(n)
    ang = 2.0 * np.pi * ((j[:, None] * j[None, :]) % n) / n
    table = np.concatenate([np.cos(ang), sin_sign * np.sin(ang)], axis=1).astype(np.float32)
    return jnp.asarray(table).astype(BF16)


def _channel_dft():
    return _dft_matrix(FOURIER_GROUP, 1.0)


def _sequence_dft():
    return _dft_matrix(SEQ, -1.0)


def _rope_tables():
    rows = SEQ // GRID_W
    row = jnp.broadcast_to(jnp.arange(rows)[:, None], (rows, GRID_W)).reshape(-1).astype(F32)
    col = jnp.broadcast_to(jnp.arange(GRID_W)[None, :], (rows, GRID_W)).reshape(-1).astype(F32)
    inv_freq = ROPE_THETA ** (-jnp.arange(0, ROPE_AXIS_DIM, 2, dtype=F32) / ROPE_AXIS_DIM)
    ang_r = row[:, None] * inv_freq[None, :]
    ang_c = col[:, None] * inv_freq[None, :]
    zero = jnp.zeros_like(ang_r)
    cos = jnp.concatenate([jnp.cos(ang_r)] * 2 + [jnp.cos(ang_c)] * 2, axis=-1)
    sin_first = jnp.concatenate([-jnp.sin(ang_r), zero, -jnp.sin(ang_c), zero], axis=-1)
    sin_second = jnp.concatenate([zero, jnp.sin(ang_r), zero, jnp.sin(ang_c)], axis=-1)
    per_token = lambda t: jnp.repeat(t, BATCH, axis=0)
    return per_token(cos), per_token(sin_first), per_token(sin_second)


def _even_layer(x, norm_g, w_in, fourier_w, q_gain, k_gain, w_out, tables, final_g):
    rope, cdft, seq_dft = tables
    xcs, fgate, q, k, v, agate = _even_in(
        x, norm_g.reshape(1, D_MODEL), w_in.astype(BF16), q_gain.reshape(1, HEAD_DIM),
        k_gain.reshape(1, HEAD_DIM), rope, cdft)
    f_out = _fourier_mix(xcs, seq_dft, fourier_w.astype(BF16), fgate)
    a_out = _attention(q, k, v, agate)
    return _out_proj(x, [f_out, a_out], w_out.astype(BF16), final_g)


def _odd_layer(x, norm_g, w_in, conv_w, conv_b, w_a, b_a, w_x, b_x, lam, w_out, final_g):
    xr, gate = _odd_in(x, norm_g.reshape(1, D_MODEL), w_in.astype(BF16))
    gate_w = jnp.concatenate([w_a, w_x], axis=-1).astype(BF16)
    gate_b = jnp.concatenate([b_a, b_x], axis=-1).reshape(2 * N_RNN_BLOCKS, 1, 2 * RNN_BLOCK)
    y = _rglru_mix(xr, gate, conv_w, conv_b.reshape(1, RNN_WIDTH), gate_w, gate_b,
                   lam.reshape(2 * N_RNN_BLOCKS, 1, RNN_BLOCK))
    return _out_proj(x, [y], w_out.astype(BF16), final_g)


def kernel(x, even_norm, even_w_in, fourier_w, q_gain, k_gain, even_w_out, odd_norm, odd_w_in,
           conv_w, conv_b, gate_a_w, gate_a_b, gate_x_w, gate_x_b, rglru_lambda, odd_w_out,
           final_norm):
    tables = (_rope_tables(), _channel_dft(), _sequence_dft())
    h = jnp.transpose(x, (1, 0, 2)).reshape(TOKENS, D_MODEL)
    for layer in range(DEPTH):
        i = layer // 2
        final_g = final_norm.reshape(1, D_MODEL) if layer == DEPTH - 1 else None
        if layer % 2 == 0:
            h = _even_layer(h, even_norm[i], even_w_in[i], fourier_w[i], q_gain[i], k_gain[i],
                            even_w_out[i], tables, final_g)
        else:
            h = _odd_layer(h, odd_norm[i], odd_w_in[i], conv_w[i], conv_b[i], gate_a_w[i],
                           gate_a_b[i], gate_x_w[i], gate_x_b[i], rglru_lambda[i], odd_w_out[i],
                           final_g)
    return jnp.transpose(h.reshape(SEQ, BATCH, D_MODEL), (1, 0, 2))
```

```python
import functools

import jax
import jax.numpy as jnp
import numpy as np
from jax import lax
from jax.experimental import pallas as pl
from jax.experimental.pallas import tpu as pltpu

D_MODEL = 1024
BATCH = 16
SEQ = 2048
DEPTH = 4
TOKENS = BATCH * SEQ

GRID_W = 64
MIX_WIDTH = 2 * D_MODEL
FOURIER_WIDTH = MIX_WIDTH // 2
N_FOURIER_GROUPS = 4
FOURIER_GROUP = FOURIER_WIDTH // N_FOURIER_GROUPS
ATTN_WIDTH = MIX_WIDTH - FOURIER_WIDTH
HEAD_DIM = 128
N_Q_HEADS = ATTN_WIDTH // HEAD_DIM
N_KV_HEADS = 2
Q_PER_KV = N_Q_HEADS // N_KV_HEADS
KV_WIDTH = N_KV_HEADS * HEAD_DIM
ROPE_THETA = 10000.0
ROPE_AXIS_DIM = HEAD_DIM // 2
RNN_WIDTH = MIX_WIDTH
N_RNN_BLOCKS = 16
RNN_BLOCK = RNN_WIDTH // N_RNN_BLOCKS
CONV_WIDTH = 4
CONV_LEFT = 2
RG_LRU_C = 8.0
NORM_EPS = 1e-6
EVEN_IN_WIDTH = 2 * FOURIER_WIDTH + ATTN_WIDTH + 2 * KV_WIDTH + ATTN_WIDTH

F32 = jnp.float32
BF16 = jnp.bfloat16
F32_MIN_NORMAL = float(np.finfo(np.float32).tiny)

VMEM_LIMIT_BYTES = 56 * 1024 * 1024

ROW_TILE = 512
SEQ_TILES = SEQ // ROW_TILE
ATTN_Q_TILE = 256
SCAN_CHUNK_STEPS = 128
SCAN_CHUNK_ROWS = SCAN_CHUNK_STEPS * BATCH
N_SCAN_CHUNKS = SEQ // SCAN_CHUNK_STEPS
SCAN_UNROLL = 8
HALO_PREV_ROWS = CONV_LEFT * BATCH
HALO_NEXT_ROWS = (CONV_WIDTH - 1 - CONV_LEFT) * BATCH

QK_SCALE = float(HEAD_DIM ** -0.5 * np.log2(np.e))


def _resident(block_shape, index_map):
    return pl.BlockSpec(block_shape, index_map, pipeline_mode=pl.Buffered(1))


def _params(n_axes):
    return pltpu.CompilerParams(
        dimension_semantics=("arbitrary",) * n_axes,
        vmem_limit_bytes=VMEM_LIMIT_BYTES)


def _rms_norm(x, gain):
    return x * lax.rsqrt(jnp.mean(x * x, axis=-1, keepdims=True) + NORM_EPS) * gain


def _silu(x):
    return x * jax.nn.sigmoid(x)


def _bm_rows(width):
    return pl.BlockSpec((ROW_TILE, width), lambda b, i: (b * SEQ_TILES + i, 0))


def _even_in_kernel(x_ref, g_ref, w_ref, qg_ref, kg_ref, cos_ref, sina_ref, sinb_ref,
                    cdft_ref, xcs_ref, fgate_ref, q_ref, k_ref, v_ref, agate_ref):
    hb = _rms_norm(x_ref[...], g_ref[...]).astype(BF16)

    def proj(lo, width):
        return jnp.dot(hb, w_ref[:, lo:lo + width], preferred_element_type=F32)

    f_in = proj(0, FOURIER_WIDTH).astype(BF16)
    for g in range(N_FOURIER_GROUPS):
        lo = g * FOURIER_GROUP
        z = jnp.dot(f_in[:, lo:lo + FOURIER_GROUP], cdft_ref[...], preferred_element_type=F32)
        xcs_ref[0, :, lo:lo + FOURIER_GROUP] = z[:, :FOURIER_GROUP].astype(BF16)
        xcs_ref[1, :, lo:lo + FOURIER_GROUP] = z[:, FOURIER_GROUP:].astype(BF16)
    fgate_ref[...] = _silu(proj(FOURIER_WIDTH, FOURIER_WIDTH))

    cos = cos_ref[...]
    sina = sina_ref[...]
    sinb = sinb_ref[...]

    def norm_rope(y, gain):
        yn = _rms_norm(y, gain)
        return (yn * cos + pltpu.roll(yn, HEAD_DIM - ROPE_AXIS_DIM // 2, 1) * sina
                + pltpu.roll(yn, ROPE_AXIS_DIM // 2, 1) * sinb)

    q_lo = 2 * FOURIER_WIDTH
    q = proj(q_lo, ATTN_WIDTH)
    for h in range(N_Q_HEADS):
        lo = h * HEAD_DIM
        qh = norm_rope(q[:, lo:lo + HEAD_DIM], qg_ref[...]) * QK_SCALE
        q_ref[:, lo:lo + HEAD_DIM] = qh.astype(BF16)
    kv = proj(q_lo + ATTN_WIDTH, 2 * KV_WIDTH)
    for h in range(N_KV_HEADS):
        lo = h * HEAD_DIM
        k_ref[:, lo:lo + HEAD_DIM] = norm_rope(kv[:, lo:lo + HEAD_DIM], kg_ref[...]).astype(BF16)
    v_ref[...] = kv[:, KV_WIDTH:].astype(BF16)
    agate_ref[...] = _silu(proj(q_lo + ATTN_WIDTH + 2 * KV_WIDTH, ATTN_WIDTH))


def _even_in(x, norm_g, w_in, q_gain, k_gain, rope, cdft):
    fixed = lambda b, i: (0, 0)
    table = pl.BlockSpec((ROW_TILE, HEAD_DIM), lambda b, i: (i, 0))
    return pl.pallas_call(
        _even_in_kernel,
        grid=(BATCH, SEQ_TILES),
        in_specs=[
            _bm_rows(D_MODEL),
            _resident((1, D_MODEL), fixed),
            _resident((D_MODEL, EVEN_IN_WIDTH), fixed),
            _resident((1, HEAD_DIM), fixed),
            _resident((1, HEAD_DIM), fixed),
            table, table, table,
            _resident((FOURIER_GROUP, 2 * FOURIER_GROUP), fixed),
        ],
        out_specs=[
            pl.BlockSpec((2, ROW_TILE, FOURIER_WIDTH), lambda b, i: (0, b * SEQ_TILES + i, 0)),
            _bm_rows(FOURIER_WIDTH),
            _bm_rows(ATTN_WIDTH),
            _bm_rows(KV_WIDTH),
            _bm_rows(KV_WIDTH),
            _bm_rows(ATTN_WIDTH),
        ],
        out_shape=[
            jax.ShapeDtypeStruct((2, TOKENS, FOURIER_WIDTH), BF16),
            jax.ShapeDtypeStruct((TOKENS, FOURIER_WIDTH), F32),
            jax.ShapeDtypeStruct((TOKENS, ATTN_WIDTH), BF16),
            jax.ShapeDtypeStruct((TOKENS, KV_WIDTH), BF16),
            jax.ShapeDtypeStruct((TOKENS, KV_WIDTH), BF16),
            jax.ShapeDtypeStruct((TOKENS, ATTN_WIDTH), F32),
        ],
        compiler_params=_params(2),
        name="even_in_proj",
    )(x, norm_g, w_in, q_gain, k_gain, *rope, cdft)


def _fourier_kernel(xcs_ref, dft_ref, fw_ref, gate_ref, o_ref):
    rhs = xcs_ref[...].reshape(2 * SEQ, FOURIER_GROUP)
    y = jnp.dot(dft_ref[...], rhs, preferred_element_type=F32)
    y = y * np.float32(1.0 / np.sqrt(SEQ * FOURIER_GROUP))
    fm = jnp.dot(y.astype(BF16), fw_ref[0], preferred_element_type=F32)
    o_ref[...] = (fm * gate_ref[...]).astype(BF16)


def _fourier_mix(xcs, seq_dft, fourier_w, fgate):
    blk = lambda b, g: (b, g)
    return pl.pallas_call(
        _fourier_kernel,
        grid=(BATCH, N_FOURIER_GROUPS),
        in_specs=[
            pl.BlockSpec((2, SEQ, FOURIER_GROUP), lambda b, g: (0, b, g)),
            _resident((SEQ, 2 * SEQ), lambda b, g: (0, 0)),
            pl.BlockSpec((1, FOURIER_GROUP, FOURIER_GROUP), lambda b, g: (g, 0, 0)),
            pl.BlockSpec((SEQ, FOURIER_GROUP), blk),
        ],
        out_specs=pl.BlockSpec((SEQ, FOURIER_GROUP), blk),
        out_shape=jax.ShapeDtypeStruct((TOKENS, FOURIER_WIDTH), BF16),
        compiler_params=_params(2),
        name="fourier_mix",
    )(xcs, seq_dft, fourier_w, fgate)


def _attention_kernel(q_ref, k_ref, v_ref, gate_ref, o_ref, vaug_ref):
    vaug_ref[:, :HEAD_DIM] = v_ref[...]
    vaug_ref[:, HEAD_DIM:] = jnp.ones((SEQ, HEAD_DIM), BF16)
    k = k_ref[...]

    def q_tile(i, carry):
        r = pl.multiple_of(i * ATTN_Q_TILE, ATTN_Q_TILE)
        rows = pl.ds(r, ATTN_Q_TILE)
        for h in range(Q_PER_KV):
            cols = slice(h * HEAD_DIM, (h + 1) * HEAD_DIM)
            s = lax.dot_general(q_ref[rows, cols], k, (((1,), (1,)), ((), ())),
                                preferred_element_type=F32)
            p = jnp.exp2(s - jnp.max(s, axis=-1, keepdims=True)).astype(BF16)
            ol = jnp.dot(p, vaug_ref[...], preferred_element_type=F32)
            o = ol[:, :HEAD_DIM] / ol[:, HEAD_DIM:]
            o_ref[rows, cols] = (o * gate_ref[rows, cols]).astype(BF16)
        return carry

    lax.fori_loop(0, SEQ // ATTN_Q_TILE, q_tile, 0)


def _attention(q, k, v, agate):
    group_width = Q_PER_KV * HEAD_DIM
    blk = lambda b, h: (b, h)
    return pl.pallas_call(
        _attention_kernel,
        grid=(BATCH, N_KV_HEADS),
        in_specs=[
            pl.BlockSpec((SEQ, group_width), blk),
            pl.BlockSpec((SEQ, HEAD_DIM), blk),
            pl.BlockSpec((SEQ, HEAD_DIM), blk),
            pl.BlockSpec((SEQ, group_width), blk),
        ],
        out_specs=pl.BlockSpec((SEQ, group_width), blk),
        out_shape=jax.ShapeDtypeStruct((TOKENS, ATTN_WIDTH), BF16),
        scratch_shapes=[pltpu.VMEM((SEQ, 2 * HEAD_DIM), BF16)],
        compiler_params=_params(2),
        name="gqa_attention",
    )(q, k, v, agate)


def _out_proj_kernel(*refs, n_mix, final_norm):
    x_ref = refs[0]
    mix_refs = refs[1:1 + n_mix]
    w_ref = refs[1 + n_mix]
    o_ref = refs[-1]
    acc = x_ref[...]
    lo = 0
    for m_ref in mix_refs:
        width = m_ref.shape[1]
        acc = acc + jnp.dot(m_ref[...], w_ref[lo:lo + width, :], preferred_element_type=F32)
        lo += width
    if final_norm:
        acc = _rms_norm(acc, refs[2 + n_mix][...])
    o_ref[...] = acc


def _even_out_proj(x, mixes, w_out):
    return pl.pallas_call(
        functools.partial(_out_proj_kernel, n_mix=len(mixes), final_norm=False),
        grid=(BATCH, SEQ_TILES),
        in_specs=[_bm_rows(D_MODEL)] + [_bm_rows(m.shape[1]) for m in mixes]
                 + [_resident((MIX_WIDTH, D_MODEL), lambda b, i: (0, 0))],
        out_specs=_bm_rows(D_MODEL),
        out_shape=jax.ShapeDtypeStruct((TOKENS, D_MODEL), F32),
        compiler_params=_params(2),
        name="even_out_proj",
    )(x, *mixes, w_out)


def _odd_out_proj(x, y, w_out, final_g):
    row = lambda i: (i, 0)
    fixed = lambda i: (0, 0)
    in_specs = [pl.BlockSpec((ROW_TILE, D_MODEL), row), pl.BlockSpec((ROW_TILE, RNN_WIDTH), row),
                _resident((RNN_WIDTH, D_MODEL), fixed)]
    args = [x, y, w_out]
    if final_g is not None:
        in_specs += [_resident((1, D_MODEL), fixed)]
        args += [final_g]
    return pl.pallas_call(
        functools.partial(_out_proj_kernel, n_mix=1, final_norm=final_g is not None),
        grid=(TOKENS // ROW_TILE,),
        in_specs=in_specs,
        out_specs=pl.BlockSpec((ROW_TILE, D_MODEL), row),
        out_shape=jax.ShapeDtypeStruct((TOKENS, D_MODEL), F32),
        compiler_params=_params(1),
        name="odd_out_proj",
    )(*args)


def _odd_in_kernel(x_ref, g_ref, w_ref, xr_ref, gate_ref):
    hb = _rms_norm(x_ref[...], g_ref[...]).astype(BF16)
    xr_ref[...] = jnp.dot(hb, w_ref[:, :RNN_WIDTH], preferred_element_type=F32)
    gate_ref[...] = _silu(jnp.dot(hb, w_ref[:, RNN_WIDTH:], preferred_element_type=F32))


def _odd_in(x, norm_g, w_in):
    row = lambda i: (i, 0)
    fixed = lambda i: (0, 0)
    return pl.pallas_call(
        _odd_in_kernel,
        grid=(TOKENS // ROW_TILE,),
        in_specs=[
            pl.BlockSpec((ROW_TILE, D_MODEL), row),
            _resident((1, D_MODEL), fixed),
            _resident((D_MODEL, 2 * RNN_WIDTH), fixed),
        ],
        out_specs=[pl.BlockSpec((ROW_TILE, RNN_WIDTH), row)] * 2,
        out_shape=[jax.ShapeDtypeStruct((TOKENS, RNN_WIDTH), F32)] * 2,
        compiler_params=_params(1),
        name="odd_in_proj",
    )(x, norm_g, w_in)


def _scan_terms(zh, xh, half_log_decay):
    t_r = jnp.tanh(zh[:, :RNN_BLOCK])
    t_i = jnp.tanh(zh[:, RNN_BLOCK:])
    log_a = half_log_decay * t_r + half_log_decay
    a = jnp.exp(log_a)
    w = jnp.tanh(log_a) * (-1.0 - a * a)
    sqrt_w = w * lax.rsqrt(jnp.maximum(w, F32_MIN_NORMAL))
    u = sqrt_w * (xh * t_i + xh)
    return a, u


def _rglru_kernel(xr_ref, prev_ref, next_ref, cw_ref, cb_ref, gw_ref, gb_ref, lam_ref,
                  gate_ref, y_ref, ext_ref, xh_all_ref, hf_all_ref, zh_ref, h_ref):
    phase = pl.program_id(1)
    j = pl.program_id(2)
    n_rows = SCAN_CHUNK_ROWS

    @pl.when(j == 0)
    def _():
        h_ref[...] = jnp.zeros_like(h_ref)

    half_log_decay = jnp.broadcast_to(
        (-0.5 * RG_LRU_C) * jax.nn.softplus(-lam_ref[0]), (BATCH, RNN_BLOCK))

    def gates(base):
        xh = xh_all_ref[pl.ds(base, n_rows), :]
        zh_ref[...] = (jnp.dot(xh.astype(BF16), gw_ref[0, 0], preferred_element_type=F32)
                       + 0.5 * gb_ref[0])

    def scan_step(base, t, h):
        rows = pl.ds(pl.multiple_of(t * BATCH, BATCH), BATCH)
        rows_all = pl.ds(pl.multiple_of(base + t * BATCH, BATCH), BATCH)
        a, u = _scan_terms(zh_ref[rows, :], xh_all_ref[rows_all, :], half_log_decay)
        return rows, rows_all, a * h + u

    @pl.when(phase == 0)
    def _():
        base = pl.multiple_of(j * n_rows, n_rows)
        ext_ref[:HALO_PREV_ROWS, :] = jnp.where(j > 0, prev_ref[...], 0.0)
        ext_ref[HALO_PREV_ROWS:HALO_PREV_ROWS + n_rows, :] = xr_ref[...]
        ext_ref[HALO_PREV_ROWS + n_rows:, :] = jnp.where(j < N_SCAN_CHUNKS - 1, next_ref[...], 0.0)
        half_w = 0.5 * cw_ref[...]
        xh = 0.5 * cb_ref[...] + ext_ref[0:n_rows, :] * half_w[0:1, :]
        for tap in range(1, CONV_WIDTH):
            xh = xh + ext_ref[tap * BATCH:tap * BATCH + n_rows, :] * half_w[tap:tap + 1, :]
        xh_all_ref[pl.ds(base, n_rows), :] = xh
        gates(base)

        def step(t, h):
            _, rows_all, h = scan_step(base, t, h)
            hf_all_ref[rows_all, :] = h
            return h

        h_ref[...] = lax.fori_loop(0, SCAN_CHUNK_STEPS, step, h_ref[...], unroll=SCAN_UNROLL)

    @pl.when(phase == 1)
    def _():
        base = pl.multiple_of((N_SCAN_CHUNKS - 1 - j) * n_rows, n_rows)
        gates(base)

        def step(i, h):
            rows, rows_all, h = scan_step(base, SCAN_CHUNK_STEPS - 1 - i, h)
            y = (hf_all_ref[rows_all, :] + h) * gate_ref[rows, :]
            y_ref[rows, :] = y.astype(BF16)
            return h

        h_ref[...] = lax.fori_loop(0, SCAN_CHUNK_STEPS, step, h_ref[...], unroll=SCAN_UNROLL)


def _rglru_mix(xr, gate, conv_w, conv_b, gate_w, gate_b, lam):
    n = N_SCAN_CHUNKS
    rows = SCAN_CHUNK_ROWS
    prev_per_chunk = rows // HALO_PREV_ROWS
    next_per_chunk = rows // HALO_NEXT_ROWS
    last_next = TOKENS // HALO_NEXT_ROWS - 1

    def fwd_chunk(p, j):
        return j * (1 - p) + (n - 1) * p

    def bwd_chunk(p, j):
        return (n - 1) - j * p

    return pl.pallas_call(
        _rglru_kernel,
        grid=(N_RNN_BLOCKS, 2, n),
        in_specs=[
            pl.BlockSpec((rows, RNN_BLOCK), lambda c, p, j: (fwd_chunk(p, j), c)),
            pl.BlockSpec((HALO_PREV_ROWS, RNN_BLOCK),
                         lambda c, p, j: (jnp.maximum(fwd_chunk(p, j) * prev_per_chunk - 1, 0), c)),
            pl.BlockSpec((HALO_NEXT_ROWS, RNN_BLOCK),
                         lambda c, p, j: (jnp.minimum((fwd_chunk(p, j) + 1) * next_per_chunk, last_next), c)),
            pl.BlockSpec((CONV_WIDTH, RNN_BLOCK), lambda c, p, j: (0, c)),
            pl.BlockSpec((1, RNN_BLOCK), lambda c, p, j: (0, c)),
            pl.BlockSpec((1, 1, RNN_BLOCK, 2 * RNN_BLOCK), lambda c, p, j: (p, c, 0, 0)),
            pl.BlockSpec((1, 1, 2 * RNN_BLOCK), lambda c, p, j: (p * N_RNN_BLOCKS + c, 0, 0)),
            pl.BlockSpec((1, 1, RNN_BLOCK), lambda c, p, j: (p * N_RNN_BLOCKS + c, 0, 0)),
            pl.BlockSpec((rows, RNN_BLOCK), lambda c, p, j: (bwd_chunk(p, j), c)),
        ],
        out_specs=pl.BlockSpec((rows, RNN_BLOCK), lambda c, p, j: (bwd_chunk(p, j), c)),
        out_shape=jax.ShapeDtypeStruct((TOKENS, RNN_WIDTH), BF16),
        scratch_shapes=[
            pltpu.VMEM((rows + HALO_PREV_ROWS + HALO_NEXT_ROWS, RNN_BLOCK), F32),
            pltpu.VMEM((TOKENS, RNN_BLOCK), F32),
            pltpu.VMEM((TOKENS, RNN_BLOCK), F32),
            pltpu.VMEM((rows, 2 * RNN_BLOCK), F32),
            pltpu.VMEM((BATCH, RNN_BLOCK), F32),
        ],
        compiler_params=_params(3),
        name="rglru_mix",
    )(xr, xr, xr, conv_w, conv_b, gate_w, gate_b, lam, gate)


def _dft_matrix(n, sin_sign):
    j = np.arange(n)
    ang = 2.0 * np.pi * ((j[:, None] * j[None, :]) % n) / n
    table = np.concatenate([np.cos(ang), sin_sign * np.sin(ang)], axis=1).astype(np.float32)
    return jnp.asarray(table).astype(BF16)


def _channel_dft():
    return _dft_matrix(FOURIER_GROUP, 1.0)


def _sequence_dft():
    return _dft_matrix(SEQ, -1.0)


def _rope_tables():
    rows = SEQ // GRID_W
    row = jnp.broadcast_to(jnp.arange(rows)[:, None], (rows, GRID_W)).reshape(-1).astype(F32)
    col = jnp.broadcast_to(jnp.arange(GRID_W)[None, :], (rows, GRID_W)).reshape(-1).astype(F32)
    inv_freq = ROPE_THETA ** (-jnp.arange(0, ROPE_AXIS_DIM, 2, dtype=F32) / ROPE_AXIS_DIM)
    ang_r = row[:, None] * inv_freq[None, :]
    ang_c = col[:, None] * inv_freq[None, :]
    zero = jnp.zeros_like(ang_r)
    cos = jnp.concatenate([jnp.cos(ang_r)] * 2 + [jnp.cos(ang_c)] * 2, axis=-1)
    sin_first = jnp.concatenate([-jnp.sin(ang_r), zero, -jnp.sin(ang_c), zero], axis=-1)
    sin_second = jnp.concatenate([zero, jnp.sin(ang_r), zero, jnp.sin(ang_c)], axis=-1)
    return cos, sin_first, sin_second


def _even_layer(x, norm_g, w_in, fourier_w, q_gain, k_gain, w_out, tables):
    rope, cdft, seq_dft = tables
    xcs, fgate, q, k, v, agate = _even_in(
        x, norm_g.reshape(1, D_MODEL), w_in.astype(BF16),
        q_gain.reshape(1, HEAD_DIM), k_gain.reshape(1, HEAD_DIM), rope, cdft)
    f_out = _fourier_mix(xcs, seq_dft, fourier_w.astype(BF16), fgate)
    a_out = _attention(q, k, v, agate)
    return _even_out_proj(x, [f_out, a_out], w_out.astype(BF16))


def _odd_layer(x, norm_g, w_in, conv_w, conv_b, w_a, b_a, w_x, b_x, lam, w_out, final_g):
    xr, gate = _odd_in(x, norm_g.reshape(1, D_MODEL), w_in.astype(BF16))
    gate_w = jnp.concatenate([w_a, w_x], axis=-1).astype(BF16)
    gate_b = jnp.concatenate([b_a, b_x], axis=-1).reshape(2 * N_RNN_BLOCKS, 1, 2 * RNN_BLOCK)
    y = _rglru_mix(xr, gate, conv_w, conv_b.reshape(1, RNN_WIDTH), gate_w, gate_b,
                   lam.reshape(2 * N_RNN_BLOCKS, 1, RNN_BLOCK))
    return _odd_out_proj(x, y, w_out.astype(BF16), final_g)


def kernel(x, even_norm, even_w_in, fourier_w, q_gain, k_gain, even_w_out, odd_norm, odd_w_in,
           conv_w, conv_b, gate_a_w, gate_a_b, gate_x_w, gate_x_b, rglru_lambda, odd_w_out,
           final_norm):
    tables = (_rope_tables(), _channel_dft(), _sequence_dft())
    def reorder(a, major, minor):
        return jnp.transpose(a.reshape(major, minor, D_MODEL), (1, 0, 2)).reshape(TOKENS, D_MODEL)

    h = x.reshape(TOKENS, D_MODEL)
    for layer in range(DEPTH):
        i = layer // 2
        if layer % 2 == 0:
            if layer > 0:
                h = reorder(h, SEQ, BATCH)
            h = _even_layer(h, even_norm[i], even_w_in[i], fourier_w[i], q_gain[i],
                            k_gain[i], even_w_out[i], tables)
        else:
            final_g = final_norm.reshape(1, D_MODEL) if layer == DEPTH - 1 else None
            h = _odd_layer(reorder(h, BATCH, SEQ), odd_norm[i], odd_w_in[i], conv_w[i], conv_b[i], gate_a_w[i],
                           gate_a_b[i], gate_x_w[i], gate_x_b[i], rglru_lambda[i], odd_w_out[i],
                           final_g)
    return reorder(h, SEQ, BATCH).reshape(BATCH, SEQ, D_MODEL)
```

```python
import functools

import jax
import jax.numpy as jnp
import numpy as np
from jax import lax
from jax.experimental import pallas as pl
from jax.experimental.pallas import tpu as pltpu

D_MODEL = 1024
BATCH = 16
SEQ = 2048
DEPTH = 4
TOKENS = BATCH * SEQ

GRID_W = 64
MIX_WIDTH = 2 * D_MODEL
FOURIER_WIDTH = MIX_WIDTH // 2
N_FOURIER_GROUPS = 4
FOURIER_GROUP = FOURIER_WIDTH // N_FOURIER_GROUPS
ATTN_WIDTH = MIX_WIDTH - FOURIER_WIDTH
HEAD_DIM = 128
N_Q_HEADS = ATTN_WIDTH // HEAD_DIM
N_KV_HEADS = 2
Q_PER_KV = N_Q_HEADS // N_KV_HEADS
KV_WIDTH = N_KV_HEADS * HEAD_DIM
ROPE_THETA = 10000.0
ROPE_AXIS_DIM = HEAD_DIM // 2
RNN_WIDTH = MIX_WIDTH
N_RNN_BLOCKS = 16
RNN_BLOCK = RNN_WIDTH // N_RNN_BLOCKS
CONV_WIDTH = 4
CONV_LEFT = 2
RG_LRU_C = 8.0
NORM_EPS = 1e-6
EVEN_IN_WIDTH = 2 * FOURIER_WIDTH + ATTN_WIDTH + 2 * KV_WIDTH + ATTN_WIDTH

F32 = jnp.float32
BF16 = jnp.bfloat16
F32_MIN_NORMAL = float(np.finfo(np.float32).tiny)

VMEM_LIMIT_BYTES = 56 * 1024 * 1024

ROW_TILE = 512
SEQ_TILES = SEQ // ROW_TILE
ATTN_Q_TILE = 512
SCAN_CHUNK_STEPS = 128
SCAN_CHUNK_ROWS = SCAN_CHUNK_STEPS * BATCH
N_SCAN_CHUNKS = SEQ // SCAN_CHUNK_STEPS
SCAN_UNROLL = 8
GATE_ROWS = 512
HALO_PREV_ROWS = CONV_LEFT * BATCH
HALO_NEXT_ROWS = (CONV_WIDTH - 1 - CONV_LEFT) * BATCH

QK_SCALE = float(HEAD_DIM ** -0.5 * np.log2(np.e))


def _resident(block_shape, index_map):
    return pl.BlockSpec(block_shape, index_map, pipeline_mode=pl.Buffered(1))


def _params(n_axes):
    return pltpu.CompilerParams(
        dimension_semantics=("arbitrary",) * n_axes,
        vmem_limit_bytes=VMEM_LIMIT_BYTES)


def _rms_norm(x, gain):
    return x * lax.rsqrt(jnp.mean(x * x, axis=-1, keepdims=True) + NORM_EPS) * gain


def _silu(x):
    return x * jax.nn.sigmoid(x)


def _bm_rows(width):
    return pl.BlockSpec((ROW_TILE, width), lambda b, i: (b * SEQ_TILES + i, 0))


def _even_in_kernel(x_ref, g_ref, w_ref, qg_ref, kg_ref, cos_ref, sina_ref, sinb_ref,
                    cdft_ref, xcs_ref, fgate_ref, q_ref, k_ref, v_ref, agate_ref):
    hb = _rms_norm(x_ref[...], g_ref[...]).astype(BF16)

    def proj(lo, width):
        return jnp.dot(hb, w_ref[:, lo:lo + width], preferred_element_type=F32)

    f_in = proj(0, FOURIER_WIDTH).astype(BF16)
    for g in range(N_FOURIER_GROUPS):
        lo = g * FOURIER_GROUP
        z = jnp.dot(f_in[:, lo:lo + FOURIER_GROUP], cdft_ref[...], preferred_element_type=F32)
        xcs_ref[0, :, lo:lo + FOURIER_GROUP] = z[:, :FOURIER_GROUP].astype(BF16)
        xcs_ref[1, :, lo:lo + FOURIER_GROUP] = z[:, FOURIER_GROUP:].astype(BF16)
    fgate_ref[...] = _silu(proj(FOURIER_WIDTH, FOURIER_WIDTH))

    cos = cos_ref[...]
    sina = sina_ref[...]
    sinb = sinb_ref[...]

    def norm_rope(y, gain):
        yn = _rms_norm(y, gain)
        return (yn * cos + pltpu.roll(yn, HEAD_DIM - ROPE_AXIS_DIM // 2, 1) * sina
                + pltpu.roll(yn, ROPE_AXIS_DIM // 2, 1) * sinb)

    q_lo = 2 * FOURIER_WIDTH
    q = proj(q_lo, ATTN_WIDTH)
    for h in range(N_Q_HEADS):
        lo = h * HEAD_DIM
        qh = norm_rope(q[:, lo:lo + HEAD_DIM], qg_ref[...]) * QK_SCALE
        q_ref[:, lo:lo + HEAD_DIM] = qh.astype(BF16)
    kv = proj(q_lo + ATTN_WIDTH, 2 * KV_WIDTH)
    for h in range(N_KV_HEADS):
        lo = h * HEAD_DIM
        k_ref[:, lo:lo + HEAD_DIM] = norm_rope(kv[:, lo:lo + HEAD_DIM], kg_ref[...]).astype(BF16)
    v_ref[...] = kv[:, KV_WIDTH:].astype(BF16)
    agate_ref[...] = _silu(proj(q_lo + ATTN_WIDTH + 2 * KV_WIDTH, ATTN_WIDTH))


def _even_in(x, norm_g, w_in, q_gain, k_gain, rope, cdft):
    fixed = lambda b, i: (0, 0)
    table = pl.BlockSpec((ROW_TILE, HEAD_DIM), lambda b, i: (i, 0))
    return pl.pallas_call(
        _even_in_kernel,
        grid=(BATCH, SEQ_TILES),
        in_specs=[
            _bm_rows(D_MODEL),
            _resident((1, D_MODEL), fixed),
            _resident((D_MODEL, EVEN_IN_WIDTH), fixed),
            _resident((1, HEAD_DIM), fixed),
            _resident((1, HEAD_DIM), fixed),
            table, table, table,
            _resident((FOURIER_GROUP, 2 * FOURIER_GROUP), fixed),
        ],
        out_specs=[
            pl.BlockSpec((2, ROW_TILE, FOURIER_WIDTH), lambda b, i: (0, b * SEQ_TILES + i, 0)),
            _bm_rows(FOURIER_WIDTH),
            _bm_rows(ATTN_WIDTH),
            _bm_rows(KV_WIDTH),
            _bm_rows(KV_WIDTH),
            _bm_rows(ATTN_WIDTH),
        ],
        out_shape=[
            jax.ShapeDtypeStruct((2, TOKENS, FOURIER_WIDTH), BF16),
            jax.ShapeDtypeStruct((TOKENS, FOURIER_WIDTH), F32),
            jax.ShapeDtypeStruct((TOKENS, ATTN_WIDTH), BF16),
            jax.ShapeDtypeStruct((TOKENS, KV_WIDTH), BF16),
            jax.ShapeDtypeStruct((TOKENS, KV_WIDTH), BF16),
            jax.ShapeDtypeStruct((TOKENS, ATTN_WIDTH), F32),
        ],
        compiler_params=_params(2),
        name="even_in_proj",
    )(x, norm_g, w_in, q_gain, k_gain, *rope, cdft)


def _fourier_kernel(xcs_ref, dft_ref, fw_ref, gate_ref, o_ref):
    rhs = xcs_ref[...].reshape(2 * SEQ, FOURIER_GROUP)
    y = jnp.dot(dft_ref[...], rhs, preferred_element_type=F32)
    y = y * np.float32(1.0 / np.sqrt(SEQ * FOURIER_GROUP))
    fm = jnp.dot(y.astype(BF16), fw_ref[0], preferred_element_type=F32)
    o_ref[...] = (fm * gate_ref[...]).astype(BF16)


def _fourier_mix(xcs, seq_dft, fourier_w, fgate):
    blk = lambda b, g: (b, g)
    return pl.pallas_call(
        _fourier_kernel,
        grid=(BATCH, N_FOURIER_GROUPS),
        in_specs=[
            pl.BlockSpec((2, SEQ, FOURIER_GROUP), lambda b, g: (0, b, g)),
            _resident((SEQ, 2 * SEQ), lambda b, g: (0, 0)),
            pl.BlockSpec((1, FOURIER_GROUP, FOURIER_GROUP), lambda b, g: (g, 0, 0)),
            pl.BlockSpec((SEQ, FOURIER_GROUP), blk),
        ],
        out_specs=pl.BlockSpec((SEQ, FOURIER_GROUP), blk),
        out_shape=jax.ShapeDtypeStruct((TOKENS, FOURIER_WIDTH), BF16),
        compiler_params=_params(2),
        name="fourier_mix",
    )(xcs, seq_dft, fourier_w, fgate)


def _attention_kernel(q_ref, k_ref, v_ref, gate_ref, o_ref, vaug_ref):
    vaug_ref[:, :HEAD_DIM] = v_ref[...]
    vaug_ref[:, HEAD_DIM:] = jnp.ones((SEQ, HEAD_DIM), BF16)
    k = k_ref[...]

    def q_tile(i, carry):
        r = pl.multiple_of(i * ATTN_Q_TILE, ATTN_Q_TILE)
        rows = pl.ds(r, ATTN_Q_TILE)
        for h in range(Q_PER_KV):
            cols = slice(h * HEAD_DIM, (h + 1) * HEAD_DIM)
            s = lax.dot_general(q_ref[rows, cols], k, (((1,), (1,)), ((), ())),
                                preferred_element_type=F32)
            p = jnp.exp2(s - jnp.max(s, axis=-1, keepdims=True)).astype(BF16)
            ol = jnp.dot(p, vaug_ref[...], preferred_element_type=F32)
            o = ol[:, :HEAD_DIM] / ol[:, HEAD_DIM:]
            o_ref[rows, cols] = (o * gate_ref[rows, cols]).astype(BF16)
        return carry

    lax.fori_loop(0, SEQ // ATTN_Q_TILE, q_tile, 0)


def _attention(q, k, v, agate):
    group_width = Q_PER_KV * HEAD_DIM
    blk = lambda b, h: (b, h)
    return pl.pallas_call(
        _attention_kernel,
        grid=(BATCH, N_KV_HEADS),
        in_specs=[
            pl.BlockSpec((SEQ, group_width), blk),
            pl.BlockSpec((SEQ, HEAD_DIM), blk),
            pl.BlockSpec((SEQ, HEAD_DIM), blk),
            pl.BlockSpec((SEQ, group_width), blk),
        ],
        out_specs=pl.BlockSpec((SEQ, group_width), blk),
        out_shape=jax.ShapeDtypeStruct((TOKENS, ATTN_WIDTH), BF16),
        scratch_shapes=[pltpu.VMEM((SEQ, 2 * HEAD_DIM), BF16)],
        compiler_params=_params(2),
        name="gqa_attention",
    )(q, k, v, agate)


def _out_proj_kernel(*refs, n_mix, final_norm):
    x_ref = refs[0]
    mix_refs = refs[1:1 + n_mix]
    w_ref = refs[1 + n_mix]
    o_ref = refs[-1]
    acc = x_ref[...]
    lo = 0
    for m_ref in mix_refs:
        width = m_ref.shape[1]
        acc = acc + jnp.dot(m_ref[...], w_ref[lo:lo + width, :], preferred_element_type=F32)
        lo += width
    if final_norm:
        acc = _rms_norm(acc, refs[2 + n_mix][...])
    o_ref[...] = acc


def _even_out_proj(x, mixes, w_out):
    return pl.pallas_call(
        functools.partial(_out_proj_kernel, n_mix=len(mixes), final_norm=False),
        grid=(BATCH, SEQ_TILES),
        in_specs=[_bm_rows(D_MODEL)] + [_bm_rows(m.shape[1]) for m in mixes]
                 + [_resident((MIX_WIDTH, D_MODEL), lambda b, i: (0, 0))],
        out_specs=_bm_rows(D_MODEL),
        out_shape=jax.ShapeDtypeStruct((TOKENS, D_MODEL), F32),
        compiler_params=_params(2),
        name="even_out_proj",
    )(x, *mixes, w_out)


def _odd_out_proj(x, y, w_out, final_g):
    row = lambda i: (i, 0)
    fixed = lambda i: (0, 0)
    in_specs = [pl.BlockSpec((ROW_TILE, D_MODEL), row), pl.BlockSpec((ROW_TILE, RNN_WIDTH), row),
                _resident((RNN_WIDTH, D_MODEL), fixed)]
    args = [x, y, w_out]
    if final_g is not None:
        in_specs += [_resident((1, D_MODEL), fixed)]
        args += [final_g]
    return pl.pallas_call(
        functools.partial(_out_proj_kernel, n_mix=1, final_norm=final_g is not None),
        grid=(TOKENS // ROW_TILE,),
        in_specs=in_specs,
        out_specs=pl.BlockSpec((ROW_TILE, D_MODEL), row),
        out_shape=jax.ShapeDtypeStruct((TOKENS, D_MODEL), F32),
        compiler_params=_params(1),
        name="odd_out_proj",
    )(*args)


def _odd_in_kernel(x_ref, g_ref, w_ref, xr_ref, gate_ref):
    hb = _rms_norm(x_ref[...], g_ref[...]).astype(BF16)
    xr_ref[...] = jnp.dot(hb, w_ref[:, :RNN_WIDTH], preferred_element_type=F32)
    gate_ref[...] = _silu(jnp.dot(hb, w_ref[:, RNN_WIDTH:], preferred_element_type=F32))


def _odd_in(x, norm_g, w_in):
    row = lambda i: (i, 0)
    fixed = lambda i: (0, 0)
    return pl.pallas_call(
        _odd_in_kernel,
        grid=(TOKENS // ROW_TILE,),
        in_specs=[
            pl.BlockSpec((ROW_TILE, D_MODEL), row),
            _resident((1, D_MODEL), fixed),
            _resident((D_MODEL, 2 * RNN_WIDTH), fixed),
        ],
        out_specs=[pl.BlockSpec((ROW_TILE, RNN_WIDTH), row)] * 2,
        out_shape=[jax.ShapeDtypeStruct((TOKENS, RNN_WIDTH), F32)] * 2,
        compiler_params=_params(1),
        name="odd_in_proj",
    )(x, norm_g, w_in)


def _rglru_kernel(xr_ref, prev_ref, next_ref, cw_ref, cb_ref, gw_ref, gb_ref, lam_ref,
                  gate_ref, y_ref, ext_ref, xh_all_ref, hf_all_ref, a_ref, u_ref, h_ref):
    phase = pl.program_id(1)
    j = pl.program_id(2)
    n_rows = SCAN_CHUNK_ROWS

    @pl.when(j == 0)
    def _():
        h_ref[...] = jnp.zeros_like(h_ref)

    def scan_terms(base):
        half_log_decay = (-0.5 * RG_LRU_C) * jax.nn.softplus(-lam_ref[0])
        half_bias = 0.5 * gb_ref[0]
        for lo in range(0, n_rows, GATE_ROWS):
            xh = xh_all_ref[pl.ds(base + lo, GATE_ROWS), :]
            zh = jnp.dot(xh.astype(BF16), gw_ref[0, 0], preferred_element_type=F32) + half_bias
            t_r = jnp.tanh(zh[:, :RNN_BLOCK])
            t_i = jnp.tanh(zh[:, RNN_BLOCK:])
            log_a = half_log_decay * t_r + half_log_decay
            a = jnp.exp(log_a)
            w = jnp.tanh(log_a) * (-1.0 - a * a)
            sqrt_w = w * lax.rsqrt(jnp.maximum(w, F32_MIN_NORMAL))
            a_ref[lo:lo + GATE_ROWS, :] = a
            u_ref[lo:lo + GATE_ROWS, :] = sqrt_w * (xh * t_i + xh)

    def scan_step(t, h):
        rows = pl.ds(pl.multiple_of(t * BATCH, BATCH), BATCH)
        return rows, a_ref[rows, :] * h + u_ref[rows, :]

    @pl.when(phase == 0)
    def _():
        base = pl.multiple_of(j * n_rows, n_rows)
        ext_ref[:HALO_PREV_ROWS, :] = jnp.where(j > 0, prev_ref[...], 0.0)
        ext_ref[HALO_PREV_ROWS:HALO_PREV_ROWS + n_rows, :] = xr_ref[...]
        ext_ref[HALO_PREV_ROWS + n_rows:, :] = jnp.where(j < N_SCAN_CHUNKS - 1, next_ref[...], 0.0)
        half_w = 0.5 * cw_ref[...]
        xh = 0.5 * cb_ref[...] + ext_ref[0:n_rows, :] * half_w[0:1, :]
        for tap in range(1, CONV_WIDTH):
            xh = xh + ext_ref[tap * BATCH:tap * BATCH + n_rows, :] * half_w[tap:tap + 1, :]
        xh_all_ref[pl.ds(base, n_rows), :] = xh
        scan_terms(base)

        def step(t, h):
            rows, h = scan_step(t, h)
            hf_all_ref[pl.ds(pl.multiple_of(base + t * BATCH, BATCH), BATCH), :] = h
            return h

        h_ref[...] = lax.fori_loop(0, SCAN_CHUNK_STEPS, step, h_ref[...], unroll=SCAN_UNROLL)

    @pl.when(phase == 1)
    def _():
        base = pl.multiple_of((N_SCAN_CHUNKS - 1 - j) * n_rows, n_rows)
        scan_terms(base)

        def step(i, h):
            t = SCAN_CHUNK_STEPS - 1 - i
            rows, h = scan_step(t, h)
            h_fwd = hf_all_ref[pl.ds(pl.multiple_of(base + t * BATCH, BATCH), BATCH), :]
            y_ref[rows, :] = ((h_fwd + h) * gate_ref[rows, :]).astype(BF16)
            return h

        h_ref[...] = lax.fori_loop(0, SCAN_CHUNK_STEPS, step, h_ref[...], unroll=SCAN_UNROLL)


def _rglru_mix(xr, gate, conv_w, conv_b, gate_w, gate_b, lam):
    n = N_SCAN_CHUNKS
    rows = SCAN_CHUNK_ROWS
    prev_per_chunk = rows // HALO_PREV_ROWS
    next_per_chunk = rows // HALO_NEXT_ROWS
    last_next = TOKENS // HALO_NEXT_ROWS - 1

    def fwd_chunk(p, j):
        return j * (1 - p) + (n - 1) * p

    def bwd_chunk(p, j):
        return (n - 1) - j * p

    return pl.pallas_call(
        _rglru_kernel,
        grid=(N_RNN_BLOCKS, 2, n),
        in_specs=[
            pl.BlockSpec((rows, RNN_BLOCK), lambda c, p, j: (fwd_chunk(p, j), c)),
            pl.BlockSpec((HALO_PREV_ROWS, RNN_BLOCK),
                         lambda c, p, j: (jnp.maximum(fwd_chunk(p, j) * prev_per_chunk - 1, 0), c)),
            pl.BlockSpec((HALO_NEXT_ROWS, RNN_BLOCK),
                         lambda c, p, j: (jnp.minimum((fwd_chunk(p, j) + 1) * next_per_chunk, last_next), c)),
            pl.BlockSpec((CONV_WIDTH, RNN_BLOCK), lambda c, p, j: (0, c)),
            pl.BlockSpec((1, RNN_BLOCK), lambda c, p, j: (0, c)),
            pl.BlockSpec((1, 1, RNN_BLOCK, 2 * RNN_BLOCK), lambda c, p, j: (p, c, 0, 0)),
            pl.BlockSpec((1, 1, 2 * RNN_BLOCK), lambda c, p, j: (p * N_RNN_BLOCKS + c, 0, 0)),
            pl.BlockSpec((1, 1, RNN_BLOCK), lambda c, p, j: (p * N_RNN_BLOCKS + c, 0, 0)),
            pl.BlockSpec((rows, RNN_BLOCK), lambda c, p, j: (bwd_chunk(p, j), c)),
        ],
        out_specs=pl.BlockSpec((rows, RNN_BLOCK), lambda c, p, j: (bwd_chunk(p, j), c)),
        out_shape=jax.ShapeDtypeStruct((TOKENS, RNN_WIDTH), BF16),
        scratch_shapes=[
            pltpu.VMEM((rows + HALO_PREV_ROWS + HALO_NEXT_ROWS, RNN_BLOCK), F32),
            pltpu.VMEM((TOKENS, RNN_BLOCK), F32),
            pltpu.VMEM((TOKENS, RNN_BLOCK), F32),
            pltpu.VMEM((rows, RNN_BLOCK), F32),
            pltpu.VMEM((rows, RNN_BLOCK), F32),
            pltpu.VMEM((BATCH, RNN_BLOCK), F32),
        ],
        compiler_params=_params(3),
        name="rglru_mix",
    )(xr, xr, xr, conv_w, conv_b, gate_w, gate_b, lam, gate)


def _dft_matrix(n, sin_sign):
    j = np.arange(n)
    ang = 2.0 * np.pi * ((j[:, None] * j[None, :]) % n) / n
    table = np.concatenate([np.cos(ang), sin_sign * np.sin(ang)], axis=1).astype(np.float32)
    return jnp.asarray(table).astype(BF16)


def _channel_dft():
    return _dft_matrix(FOURIER_GROUP, 1.0)


def _sequence_dft():
    return _dft_matrix(SEQ, -1.0)


def _rope_tables():
    rows = SEQ // GRID_W
    row = jnp.broadcast_to(jnp.arange(rows)[:, None], (rows, GRID_W)).reshape(-1).astype(F32)
    col = jnp.broadcast_to(jnp.arange(GRID_W)[None, :], (rows, GRID_W)).reshape(-1).astype(F32)
    inv_freq = ROPE_THETA ** (-jnp.arange(0, ROPE_AXIS_DIM, 2, dtype=F32) / ROPE_AXIS_DIM)
    ang_r = row[:, None] * inv_freq[None, :]
    ang_c = col[:, None] * inv_freq[None, :]
    zero = jnp.zeros_like(ang_r)
    cos = jnp.concatenate([jnp.cos(ang_r)] * 2 + [jnp.cos(ang_c)] * 2, axis=-1)
    sin_first = jnp.concatenate([-jnp.sin(ang_r), zero, -jnp.sin(ang_c), zero], axis=-1)
    sin_second = jnp.concatenate([zero, jnp.sin(ang_r), zero, jnp.sin(ang_c)], axis=-1)
    return cos, sin_first, sin_second


def _even_layer(x, norm_g, w_in, fourier_w, q_gain, k_gain, w_out, tables):
    rope, cdft, seq_dft = tables
    xcs, fgate, q, k, v, agate = _even_in(
        x, norm_g.reshape(1, D_MODEL), w_in.astype(BF16),
        q_gain.reshape(1, HEAD_DIM), k_gain.reshape(1, HEAD_DIM), rope, cdft)
    f_out = _fourier_mix(xcs, seq_dft, fourier_w.astype(BF16), fgate)
    a_out = _attention(q, k, v, agate)
    return _even_out_proj(x, [f_out, a_out], w_out.astype(BF16))


def _odd_layer(x, norm_g, w_in, conv_w, conv_b, w_a, b_a, w_x, b_x, lam, w_out, final_g):
    xr, gate = _odd_in(x, norm_g.reshape(1, D_MODEL), w_in.astype(BF16))
    gate_w = jnp.concatenate([w_a, w_x], axis=-1).astype(BF16)
    gate_b = jnp.concatenate([b_a, b_x], axis=-1).reshape(2 * N_RNN_BLOCKS, 1, 2 * RNN_BLOCK)
    y = _rglru_mix(xr, gate, conv_w, conv_b.reshape(1, RNN_WIDTH), gate_w, gate_b,
                   lam.reshape(2 * N_RNN_BLOCKS, 1, RNN_BLOCK))
    return _odd_out_proj(x, y, w_out.astype(BF16), final_g)


def kernel(x, even_norm, even_w_in, fourier_w, q_gain, k_gain, even_w_out, odd_norm, odd_w_in,
           conv_w, conv_b, gate_a_w, gate_a_b, gate_x_w, gate_x_b, rglru_lambda, odd_w_out,
           final_norm):
    tables = (_rope_tables(), _channel_dft(), _sequence_dft())
    def reorder(a, major, minor):
        return jnp.transpose(a.reshape(major, minor, D_MODEL), (1, 0, 2)).reshape(TOKENS, D_MODEL)

    h = x.reshape(TOKENS, D_MODEL)
    for layer in range(DEPTH):
        i = layer // 2
        if layer % 2 == 0:
            if layer > 0:
                h = reorder(h, SEQ, BATCH)
            h = _even_layer(h, even_norm[i], even_w_in[i], fourier_w[i], q_gain[i],
                            k_gain[i], even_w_out[i], tables)
        else:
            final_g = final_norm.reshape(1, D_MODEL) if layer == DEPTH - 1 else None
            h = _odd_layer(reorder(h, BATCH, SEQ), odd_norm[i], odd_w_in[i], conv_w[i], conv_b[i], gate_a_w[i],
                           gate_a_b[i], gate_x_w[i], gate_x_b[i], rglru_lambda[i], odd_w_out[i],
                           final_g)
    return reorder(h, SEQ, BATCH).reshape(BATCH, SEQ, D_MODEL)
```

```python
import functools

import jax
import jax.numpy as jnp
import numpy as np
from jax import lax
from jax.experimental import pallas as pl
from jax.experimental.pallas import tpu as pltpu

D_MODEL = 1024
BATCH = 16
SEQ = 2048
DEPTH = 4
TOKENS = BATCH * SEQ

GRID_W = 64
MIX_WIDTH = 2 * D_MODEL
FOURIER_WIDTH = MIX_WIDTH // 2
N_FOURIER_GROUPS = 4
FOURIER_GROUP = FOURIER_WIDTH // N_FOURIER_GROUPS
ATTN_WIDTH = MIX_WIDTH - FOURIER_WIDTH
HEAD_DIM = 128
N_Q_HEADS = ATTN_WIDTH // HEAD_DIM
N_KV_HEADS = 2
Q_PER_KV = N_Q_HEADS // N_KV_HEADS
KV_WIDTH = N_KV_HEADS * HEAD_DIM
ROPE_THETA = 10000.0
ROPE_AXIS_DIM = HEAD_DIM // 2
RNN_WIDTH = MIX_WIDTH
N_RNN_BLOCKS = 16
RNN_BLOCK = RNN_WIDTH // N_RNN_BLOCKS
CONV_WIDTH = 4
CONV_LEFT = 2
RG_LRU_C = 8.0
NORM_EPS = 1e-6
EVEN_IN_WIDTH = 2 * FOURIER_WIDTH + ATTN_WIDTH + 2 * KV_WIDTH + ATTN_WIDTH

F32 = jnp.float32
BF16 = jnp.bfloat16
F32_MIN_NORMAL = float(np.finfo(np.float32).tiny)

VMEM_LIMIT_BYTES = 56 * 1024 * 1024

ROW_TILE = 512
SEQ_TILES = SEQ // ROW_TILE
ATTN_Q_TILE = 512
SCAN_CHUNK_STEPS = 128
SCAN_CHUNK_ROWS = SCAN_CHUNK_STEPS * BATCH
N_SCAN_CHUNKS = SEQ // SCAN_CHUNK_STEPS
GATE_ROWS = 512
HALO_PREV_ROWS = CONV_LEFT * BATCH
HALO_NEXT_ROWS = (CONV_WIDTH - 1 - CONV_LEFT) * BATCH

QK_SCALE = float(HEAD_DIM ** -0.5 * np.log2(np.e))


def _resident(block_shape, index_map):
    return pl.BlockSpec(block_shape, index_map, pipeline_mode=pl.Buffered(1))


def _params(n_axes):
    return pltpu.CompilerParams(
        dimension_semantics=("arbitrary",) * n_axes,
        vmem_limit_bytes=VMEM_LIMIT_BYTES)


def _rms_norm(x, gain):
    return x * lax.rsqrt(jnp.mean(x * x, axis=-1, keepdims=True) + NORM_EPS) * gain


def _silu(x):
    return x * jax.nn.sigmoid(x)


def _bm_rows(width):
    return pl.BlockSpec((ROW_TILE, width), lambda b, i: (b * SEQ_TILES + i, 0))


def _even_in_kernel(x_ref, g_ref, w_ref, qg_ref, kg_ref, cos_ref, sina_ref, sinb_ref,
                    cdft_ref, xcs_ref, fgate_ref, q_ref, k_ref, v_ref, agate_ref):
    hb = _rms_norm(x_ref[...], g_ref[...]).astype(BF16)

    def proj(lo, width):
        return jnp.dot(hb, w_ref[:, lo:lo + width], preferred_element_type=F32)

    cos = cos_ref[...]
    sina = sina_ref[...]
    sinb = sinb_ref[...]

    def norm_rope(y, gain):
        yn = _rms_norm(y, gain)
        return (yn * cos + pltpu.roll(yn, HEAD_DIM - ROPE_AXIS_DIM // 2, 1) * sina
                + pltpu.roll(yn, ROPE_AXIS_DIM // 2, 1) * sinb)

    q_lo = 2 * FOURIER_WIDTH
    q = proj(q_lo, ATTN_WIDTH)
    for h in range(N_Q_HEADS):
        lo = h * HEAD_DIM
        qh = norm_rope(q[:, lo:lo + HEAD_DIM], qg_ref[...]) * QK_SCALE
        q_ref[:, lo:lo + HEAD_DIM] = qh.astype(BF16)
    agate_ref[...] = _silu(proj(q_lo + ATTN_WIDTH + 2 * KV_WIDTH, ATTN_WIDTH))
    k = proj(q_lo + ATTN_WIDTH, KV_WIDTH)
    for h in range(N_KV_HEADS):
        lo = h * HEAD_DIM
        k_ref[:, lo:lo + HEAD_DIM] = norm_rope(k[:, lo:lo + HEAD_DIM], kg_ref[...]).astype(BF16)
    fgate_ref[...] = _silu(proj(FOURIER_WIDTH, FOURIER_WIDTH))

    f_in = proj(0, FOURIER_WIDTH).astype(BF16)
    for g in range(N_FOURIER_GROUPS):
        lo = g * FOURIER_GROUP
        z = jnp.dot(f_in[:, lo:lo + FOURIER_GROUP], cdft_ref[...], preferred_element_type=F32)
        xcs_ref[0, :, lo:lo + FOURIER_GROUP] = z[:, :FOURIER_GROUP].astype(BF16)
        xcs_ref[1, :, lo:lo + FOURIER_GROUP] = z[:, FOURIER_GROUP:].astype(BF16)
    v_ref[...] = proj(q_lo + ATTN_WIDTH + KV_WIDTH, KV_WIDTH).astype(BF16)


def _even_in(x, norm_g, w_in, q_gain, k_gain, rope, cdft):
    fixed = lambda b, i: (0, 0)
    table = pl.BlockSpec((ROW_TILE, HEAD_DIM), lambda b, i: (i, 0))
    return pl.pallas_call(
        _even_in_kernel,
        grid=(BATCH, SEQ_TILES),
        in_specs=[
            _bm_rows(D_MODEL),
            _resident((1, D_MODEL), fixed),
            _resident((D_MODEL, EVEN_IN_WIDTH), fixed),
            _resident((1, HEAD_DIM), fixed),
            _resident((1, HEAD_DIM), fixed),
            table, table, table,
            _resident((FOURIER_GROUP, 2 * FOURIER_GROUP), fixed),
        ],
        out_specs=[
            pl.BlockSpec((2, ROW_TILE, FOURIER_WIDTH), lambda b, i: (0, b * SEQ_TILES + i, 0)),
            _bm_rows(FOURIER_WIDTH),
            _bm_rows(ATTN_WIDTH),
            _bm_rows(KV_WIDTH),
            _bm_rows(KV_WIDTH),
            _bm_rows(ATTN_WIDTH),
        ],
        out_shape=[
            jax.ShapeDtypeStruct((2, TOKENS, FOURIER_WIDTH), BF16),
            jax.ShapeDtypeStruct((TOKENS, FOURIER_WIDTH), F32),
            jax.ShapeDtypeStruct((TOKENS, ATTN_WIDTH), BF16),
            jax.ShapeDtypeStruct((TOKENS, KV_WIDTH), BF16),
            jax.ShapeDtypeStruct((TOKENS, KV_WIDTH), BF16),
            jax.ShapeDtypeStruct((TOKENS, ATTN_WIDTH), F32),
        ],
        compiler_params=_params(2),
        name="even_in_proj",
    )(x, norm_g, w_in, q_gain, k_gain, *rope, cdft)


def _fourier_kernel(xcs_ref, dft_ref, fw_ref, gate_ref, o_ref):
    rhs = xcs_ref[...].reshape(2 * SEQ, FOURIER_GROUP)
    y = jnp.dot(dft_ref[...], rhs, preferred_element_type=F32)
    y = y * np.float32(1.0 / np.sqrt(SEQ * FOURIER_GROUP))
    fm = jnp.dot(y.astype(BF16), fw_ref[0], preferred_element_type=F32)
    o_ref[...] = (fm * gate_ref[...]).astype(BF16)


def _fourier_mix(xcs, seq_dft, fourier_w, fgate):
    blk = lambda b, g: (b, g)
    return pl.pallas_call(
        _fourier_kernel,
        grid=(BATCH, N_FOURIER_GROUPS),
        in_specs=[
            pl.BlockSpec((2, SEQ, FOURIER_GROUP), lambda b, g: (0, b, g)),
            _resident((SEQ, 2 * SEQ), lambda b, g: (0, 0)),
            pl.BlockSpec((1, FOURIER_GROUP, FOURIER_GROUP), lambda b, g: (g, 0, 0)),
            pl.BlockSpec((SEQ, FOURIER_GROUP), blk),
        ],
        out_specs=pl.BlockSpec((SEQ, FOURIER_GROUP), blk),
        out_shape=jax.ShapeDtypeStruct((TOKENS, FOURIER_WIDTH), BF16),
        compiler_params=_params(2),
        name="fourier_mix",
    )(xcs, seq_dft, fourier_w, fgate)


def _attention_kernel(q_ref, k_ref, v_ref, gate_ref, o_ref, vaug_ref):
    vaug_ref[:, :HEAD_DIM] = v_ref[...]
    vaug_ref[:, HEAD_DIM:] = jnp.ones((SEQ, HEAD_DIM), BF16)
    k = k_ref[...]

    def q_tile(i, carry):
        r = pl.multiple_of(i * ATTN_Q_TILE, ATTN_Q_TILE)
        rows = pl.ds(r, ATTN_Q_TILE)
        for h in range(Q_PER_KV):
            cols = slice(h * HEAD_DIM, (h + 1) * HEAD_DIM)
            s = lax.dot_general(q_ref[rows, cols], k, (((1,), (1,)), ((), ())),
                                preferred_element_type=F32)
            p = jnp.exp2(s - jnp.max(s, axis=-1, keepdims=True)).astype(BF16)
            ol = jnp.dot(p, vaug_ref[...], preferred_element_type=F32)
            o = ol[:, :HEAD_DIM] / ol[:, HEAD_DIM:]
            o_ref[rows, cols] = (o * gate_ref[rows, cols]).astype(BF16)
        return carry

    lax.fori_loop(0, SEQ // ATTN_Q_TILE, q_tile, 0)


def _attention(q, k, v, agate):
    group_width = Q_PER_KV * HEAD_DIM
    blk = lambda b, h: (b, h)
    return pl.pallas_call(
        _attention_kernel,
        grid=(BATCH, N_KV_HEADS),
        in_specs=[
            pl.BlockSpec((SEQ, group_width), blk),
            pl.BlockSpec((SEQ, HEAD_DIM), blk),
            pl.BlockSpec((SEQ, HEAD_DIM), blk),
            pl.BlockSpec((SEQ, group_width), blk),
        ],
        out_specs=pl.BlockSpec((SEQ, group_width), blk),
        out_shape=jax.ShapeDtypeStruct((TOKENS, ATTN_WIDTH), BF16),
        scratch_shapes=[pltpu.VMEM((SEQ, 2 * HEAD_DIM), BF16)],
        compiler_params=_params(2),
        name="gqa_attention",
    )(q, k, v, agate)


def _out_proj_kernel(*refs, n_mix, final_norm):
    x_ref = refs[0]
    mix_refs = refs[1:1 + n_mix]
    w_ref = refs[1 + n_mix]
    o_ref = refs[-1]
    acc = x_ref[...]
    lo = 0
    for m_ref in mix_refs:
        width = m_ref.shape[1]
        acc = acc + jnp.dot(m_ref[...], w_ref[lo:lo + width, :], preferred_element_type=F32)
        lo += width
    if final_norm:
        acc = _rms_norm(acc, refs[2 + n_mix][...])
    o_ref[...] = acc


def _even_out_proj(x, mixes, w_out):
    return pl.pallas_call(
        functools.partial(_out_proj_kernel, n_mix=len(mixes), final_norm=False),
        grid=(BATCH, SEQ_TILES),
        in_specs=[_bm_rows(D_MODEL)] + [_bm_rows(m.shape[1]) for m in mixes]
                 + [_resident((MIX_WIDTH, D_MODEL), lambda b, i: (0, 0))],
        out_specs=_bm_rows(D_MODEL),
        out_shape=jax.ShapeDtypeStruct((TOKENS, D_MODEL), F32),
        compiler_params=_params(2),
        name="even_out_proj",
    )(x, *mixes, w_out)


def _odd_out_proj(x, y, w_out, final_g):
    row = lambda i: (i, 0)
    fixed = lambda i: (0, 0)
    in_specs = [pl.BlockSpec((ROW_TILE, D_MODEL), row), pl.BlockSpec((ROW_TILE, RNN_WIDTH), row),
                _resident((RNN_WIDTH, D_MODEL), fixed)]
    args = [x, y, w_out]
    if final_g is not None:
        in_specs += [_resident((1, D_MODEL), fixed)]
        args += [final_g]
    return pl.pallas_call(
        functools.partial(_out_proj_kernel, n_mix=1, final_norm=final_g is not None),
        grid=(TOKENS // ROW_TILE,),
        in_specs=in_specs,
        out_specs=pl.BlockSpec((ROW_TILE, D_MODEL), row),
        out_shape=jax.ShapeDtypeStruct((TOKENS, D_MODEL), F32),
        compiler_params=_params(1),
        name="odd_out_proj",
    )(*args)


def _odd_in_kernel(x_ref, g_ref, w_ref, xr_ref, gate_ref):
    hb = _rms_norm(x_ref[...], g_ref[...]).astype(BF16)
    gate_ref[...] = _silu(jnp.dot(hb, w_ref[:, RNN_WIDTH:], preferred_element_type=F32))
    xr_ref[...] = jnp.dot(hb, w_ref[:, :RNN_WIDTH], preferred_element_type=F32)


def _odd_in(x, norm_g, w_in):
    row = lambda i: (i, 0)
    fixed = lambda i: (0, 0)
    return pl.pallas_call(
        _odd_in_kernel,
        grid=(TOKENS // ROW_TILE,),
        in_specs=[
            pl.BlockSpec((ROW_TILE, D_MODEL), row),
            _resident((1, D_MODEL), fixed),
            _resident((D_MODEL, 2 * RNN_WIDTH), fixed),
        ],
        out_specs=[pl.BlockSpec((ROW_TILE, RNN_WIDTH), row)] * 2,
        out_shape=[jax.ShapeDtypeStruct((TOKENS, RNN_WIDTH), F32)] * 2,
        compiler_params=_params(1),
        name="odd_in_proj",
    )(x, norm_g, w_in)


def _rglru_kernel(xr_ref, prev_ref, next_ref, cw_ref, cb_ref, gw_ref, gb_ref, lam_ref,
                  gate_ref, y_ref, ext_ref, xh_all_ref, hf_all_ref, h_ref):
    phase = pl.program_id(1)
    j = pl.program_id(2)
    n_rows = SCAN_CHUNK_ROWS

    @pl.when(j == 0)
    def _():
        h_ref[...] = jnp.zeros_like(h_ref)

    def chunk_scan(base, reverse, emit):
        half_log_decay = (-0.5 * RG_LRU_C) * jax.nn.softplus(-lam_ref[0])
        half_bias = 0.5 * gb_ref[0]
        order = (lambda r: reversed(r)) if reverse else (lambda r: r)
        h = h_ref[...]
        for lo in order(range(0, n_rows, GATE_ROWS)):
            xh = xh_all_ref[pl.ds(base + lo, GATE_ROWS), :]
            zh = jnp.dot(xh.astype(BF16), gw_ref[0, 0], preferred_element_type=F32) + half_bias
            t_r = jnp.tanh(zh[:, :RNN_BLOCK])
            t_i = jnp.tanh(zh[:, RNN_BLOCK:])
            log_a = half_log_decay * t_r + half_log_decay
            a = jnp.exp(log_a)
            w = jnp.tanh(log_a) * (-1.0 - a * a)
            sqrt_w = w * lax.rsqrt(jnp.maximum(w, F32_MIN_NORMAL))
            u = sqrt_w * (xh * t_i + xh)
            for r in order(range(0, GATE_ROWS, BATCH)):
                h = a[r:r + BATCH, :] * h + u[r:r + BATCH, :]
                emit(lo + r, h)
        h_ref[...] = h

    @pl.when(phase == 0)
    def _():
        base = pl.multiple_of(j * n_rows, n_rows)
        ext_ref[:HALO_PREV_ROWS, :] = jnp.where(j > 0, prev_ref[...], 0.0)
        ext_ref[HALO_PREV_ROWS:HALO_PREV_ROWS + n_rows, :] = xr_ref[...]
        ext_ref[HALO_PREV_ROWS + n_rows:, :] = jnp.where(j < N_SCAN_CHUNKS - 1, next_ref[...], 0.0)
        half_w = 0.5 * cw_ref[...]
        xh = 0.5 * cb_ref[...] + ext_ref[0:n_rows, :] * half_w[0:1, :]
        for tap in range(1, CONV_WIDTH):
            xh = xh + ext_ref[tap * BATCH:tap * BATCH + n_rows, :] * half_w[tap:tap + 1, :]
        xh_all_ref[pl.ds(base, n_rows), :] = xh

        def emit(r, h):
            hf_all_ref[pl.ds(base + r, BATCH), :] = h

        chunk_scan(base, False, emit)

    @pl.when(phase == 1)
    def _():
        base = pl.multiple_of((N_SCAN_CHUNKS - 1 - j) * n_rows, n_rows)

        def emit(r, h):
            y = (hf_all_ref[pl.ds(base + r, BATCH), :] + h) * gate_ref[r:r + BATCH, :]
            y_ref[r:r + BATCH, :] = y.astype(BF16)

        chunk_scan(base, True, emit)


def _rglru_mix(xr, gate, conv_w, conv_b, gate_w, gate_b, lam):
    n = N_SCAN_CHUNKS
    rows = SCAN_CHUNK_ROWS
    prev_per_chunk = rows // HALO_PREV_ROWS
    next_per_chunk = rows // HALO_NEXT_ROWS
    last_next = TOKENS // HALO_NEXT_ROWS - 1

    def fwd_chunk(p, j):
        return j * (1 - p) + (n - 1) * p

    def bwd_chunk(p, j):
        return (n - 1) - j * p

    return pl.pallas_call(
        _rglru_kernel,
        grid=(N_RNN_BLOCKS, 2, n),
        in_specs=[
            pl.BlockSpec((rows, RNN_BLOCK), lambda c, p, j: (fwd_chunk(p, j), c)),
            pl.BlockSpec((HALO_PREV_ROWS, RNN_BLOCK),
                         lambda c, p, j: (jnp.maximum(fwd_chunk(p, j) * prev_per_chunk - 1, 0), c)),
            pl.BlockSpec((HALO_NEXT_ROWS, RNN_BLOCK),
                         lambda c, p, j: (jnp.minimum((fwd_chunk(p, j) + 1) * next_per_chunk, last_next), c)),
            pl.BlockSpec((CONV_WIDTH, RNN_BLOCK), lambda c, p, j: (0, c)),
            pl.BlockSpec((1, RNN_BLOCK), lambda c, p, j: (0, c)),
            pl.BlockSpec((1, 1, RNN_BLOCK, 2 * RNN_BLOCK), lambda c, p, j: (p, c, 0, 0)),
            pl.BlockSpec((1, 1, 2 * RNN_BLOCK), lambda c, p, j: (p * N_RNN_BLOCKS + c, 0, 0)),
            pl.BlockSpec((1, 1, RNN_BLOCK), lambda c, p, j: (p * N_RNN_BLOCKS + c, 0, 0)),
            pl.BlockSpec((rows, RNN_BLOCK), lambda c, p, j: (bwd_chunk(p, j), c)),
        ],
        out_specs=pl.BlockSpec((rows, RNN_BLOCK), lambda c, p, j: (bwd_chunk(p, j), c)),
        out_shape=jax.ShapeDtypeStruct((TOKENS, RNN_WIDTH), BF16),
        scratch_shapes=[
            pltpu.VMEM((rows + HALO_PREV_ROWS + HALO_NEXT_ROWS, RNN_BLOCK), F32),
            pltpu.VMEM((TOKENS, RNN_BLOCK), F32),
            pltpu.VMEM((TOKENS, RNN_BLOCK), F32),
            pltpu.VMEM((BATCH, RNN_BLOCK), F32),
        ],
        compiler_params=_params(3),
        name="rglru_mix",
    )(xr, xr, xr, conv_w, conv_b, gate_w, gate_b, lam, gate)


def _dft_matrix(n, sin_sign):
    j = np.arange(n)
    ang = 2.0 * np.pi * ((j[:, None] * j[None, :]) % n) / n
    table = np.concatenate([np.cos(ang), sin_sign * np.sin(ang)], axis=1).astype(np.float32)
    return jnp.asarray(table).astype(BF16)


def _channel_dft():
    return _dft_matrix(FOURIER_GROUP, 1.0)


def _sequence_dft():
    return _dft_matrix(SEQ, -1.0)


def _rope_tables():
    rows = SEQ // GRID_W
    row = jnp.broadcast_to(jnp.arange(rows)[:, None], (rows, GRID_W)).reshape(-1).astype(F32)
    col = jnp.broadcast_to(jnp.arange(GRID_W)[None, :], (rows, GRID_W)).reshape(-1).astype(F32)
    inv_freq = ROPE_THETA ** (-jnp.arange(0, ROPE_AXIS_DIM, 2, dtype=F32) / ROPE_AXIS_DIM)
    ang_r = row[:, None] * inv_freq[None, :]
    ang_c = col[:, None] * inv_freq[None, :]
    zero = jnp.zeros_like(ang_r)
    cos = jnp.concatenate([jnp.cos(ang_r)] * 2 + [jnp.cos(ang_c)] * 2, axis=-1)
    sin_first = jnp.concatenate([-jnp.sin(ang_r), zero, -jnp.sin(ang_c), zero], axis=-1)
    sin_second = jnp.concatenate([zero, jnp.sin(ang_r), zero, jnp.sin(ang_c)], axis=-1)
    return cos, sin_first, sin_second


def _even_layer(x, norm_g, w_in, fourier_w, q_gain, k_gain, w_out, tables):
    rope, cdft, seq_dft = tables
    xcs, fgate, q, k, v, agate = _even_in(
        x, norm_g.reshape(1, D_MODEL), w_in.astype(BF16),
        q_gain.reshape(1, HEAD_DIM), k_gain.reshape(1, HEAD_DIM), rope, cdft)
    f_out = _fourier_mix(xcs, seq_dft, fourier_w.astype(BF16), fgate)
    a_out = _attention(q, k, v, agate)
    return _even_out_proj(x, [f_out, a_out], w_out.astype(BF16))


def _odd_layer(x, norm_g, w_in, conv_w, conv_b, w_a, b_a, w_x, b_x, lam, w_out, final_g):
    xr, gate = _odd_in(x, norm_g.reshape(1, D_MODEL), w_in.astype(BF16))
    gate_w = jnp.concatenate([w_a, w_x], axis=-1).astype(BF16)
    gate_b = jnp.concatenate([b_a, b_x], axis=-1).reshape(2 * N_RNN_BLOCKS, 1, 2 * RNN_BLOCK)
    y = _rglru_mix(xr, gate, conv_w, conv_b.reshape(1, RNN_WIDTH), gate_w, gate_b,
                   lam.reshape(2 * N_RNN_BLOCKS, 1, RNN_BLOCK))
    return _odd_out_proj(x, y, w_out.astype(BF16), final_g)


def kernel(x, even_norm, even_w_in, fourier_w, q_gain, k_gain, even_w_out, odd_norm, odd_w_in,
           conv_w, conv_b, gate_a_w, gate_a_b, gate_x_w, gate_x_b, rglru_lambda, odd_w_out,
           final_norm):
    tables = (_rope_tables(), _channel_dft(), _sequence_dft())
    def reorder(a, major, minor):
        return jnp.transpose(a.reshape(major, minor, D_MODEL), (1, 0, 2)).reshape(TOKENS, D_MODEL)

    h = x.reshape(TOKENS, D_MODEL)
    for layer in range(DEPTH):
        i = layer // 2
        if layer % 2 == 0:
            if layer > 0:
                h = reorder(h, SEQ, BATCH)
            h = _even_layer(h, even_norm[i], even_w_in[i], fourier_w[i], q_gain[i],
                            k_gain[i], even_w_out[i], tables)
        else:
            final_g = final_norm.reshape(1, D_MODEL) if layer == DEPTH - 1 else None
            h = _odd_layer(reorder(h, BATCH, SEQ), odd_norm[i], odd_w_in[i], conv_w[i], conv_b[i], gate_a_w[i],
                           gate_a_b[i], gate_x_w[i], gate_x_b[i], rglru_lambda[i], odd_w_out[i],
                           final_g)
    return reorder(h, SEQ, BATCH).reshape(BATCH, SEQ, D_MODEL)
```

```python
import functools

import jax
import jax.numpy as jnp
import numpy as np
from jax import lax
from jax.experimental import pallas as pl
from jax.experimental.pallas import tpu as pltpu

D_MODEL = 1024
BATCH = 16
SEQ = 2048
DEPTH = 4
TOKENS = BATCH * SEQ

GRID_W = 64
MIX_WIDTH = 2 * D_MODEL
FOURIER_WIDTH = MIX_WIDTH // 2
N_FOURIER_GROUPS = 4
FOURIER_GROUP = FOURIER_WIDTH // N_FOURIER_GROUPS
ATTN_WIDTH = MIX_WIDTH - FOURIER_WIDTH
HEAD_DIM = 128
N_Q_HEADS = ATTN_WIDTH // HEAD_DIM
N_KV_HEADS = 2
Q_PER_KV = N_Q_HEADS // N_KV_HEADS
KV_WIDTH = N_KV_HEADS * HEAD_DIM
ROPE_THETA = 10000.0
ROPE_AXIS_DIM = HEAD_DIM // 2
RNN_WIDTH = MIX_WIDTH
N_RNN_BLOCKS = 16
RNN_BLOCK = RNN_WIDTH // N_RNN_BLOCKS
CONV_WIDTH = 4
CONV_LEFT = 2
RG_LRU_C = 8.0
NORM_EPS = 1e-6
EVEN_IN_WIDTH = 2 * FOURIER_WIDTH + ATTN_WIDTH + 2 * KV_WIDTH + ATTN_WIDTH

F32 = jnp.float32
BF16 = jnp.bfloat16
F32_MIN_NORMAL = float(np.finfo(np.float32).tiny)

VMEM_LIMIT_BYTES = 56 * 1024 * 1024

ROW_TILE = 512
SEQ_TILES = SEQ // ROW_TILE
OUT_ROW_TILE = 1024
ATTN_Q_TILE = 512
SCAN_CHUNK_STEPS = 256
SCAN_CHUNK_ROWS = SCAN_CHUNK_STEPS * BATCH
N_SCAN_CHUNKS = SEQ // SCAN_CHUNK_STEPS
GATE_ROWS = 512
HALO_PREV_ROWS = CONV_LEFT * BATCH
HALO_NEXT_ROWS = (CONV_WIDTH - 1 - CONV_LEFT) * BATCH

QK_SCALE = float(HEAD_DIM ** -0.5 * np.log2(np.e))


def _resident(block_shape, index_map):
    return pl.BlockSpec(block_shape, index_map, pipeline_mode=pl.Buffered(1))


def _params(n_axes):
    return pltpu.CompilerParams(
        dimension_semantics=("arbitrary",) * n_axes,
        vmem_limit_bytes=VMEM_LIMIT_BYTES)


def _rms_norm(x, gain):
    return x * lax.rsqrt(jnp.mean(x * x, axis=-1, keepdims=True) + NORM_EPS) * gain


def _silu(x):
    return x * jax.nn.sigmoid(x)


def _bm_rows(width):
    return pl.BlockSpec((ROW_TILE, width), lambda b, i: (b * SEQ_TILES + i, 0))


def _even_in_kernel(x_ref, g_ref, w_ref, qg_ref, kg_ref, cos_ref, sina_ref, sinb_ref,
                    cdft_ref, xcs_ref, fgate_ref, q_ref, k_ref, v_ref, agate_ref):
    hb = _rms_norm(x_ref[...], g_ref[...]).astype(BF16)

    def proj(lo, width):
        return jnp.dot(hb, w_ref[:, lo:lo + width], preferred_element_type=F32)

    cos = cos_ref[...]
    sina = sina_ref[...]
    sinb = sinb_ref[...]

    def norm_rope(y, gain):
        yn = _rms_norm(y, gain)
        return (yn * cos + pltpu.roll(yn, HEAD_DIM - ROPE_AXIS_DIM // 2, 1) * sina
                + pltpu.roll(yn, ROPE_AXIS_DIM // 2, 1) * sinb)

    q_lo = 2 * FOURIER_WIDTH
    q = proj(q_lo, ATTN_WIDTH)
    for h in range(N_Q_HEADS):
        lo = h * HEAD_DIM
        qh = norm_rope(q[:, lo:lo + HEAD_DIM], qg_ref[...]) * QK_SCALE
        q_ref[:, lo:lo + HEAD_DIM] = qh.astype(BF16)
    agate_ref[...] = _silu(proj(q_lo + ATTN_WIDTH + 2 * KV_WIDTH, ATTN_WIDTH))
    k = proj(q_lo + ATTN_WIDTH, KV_WIDTH)
    for h in range(N_KV_HEADS):
        lo = h * HEAD_DIM
        k_ref[:, lo:lo + HEAD_DIM] = norm_rope(k[:, lo:lo + HEAD_DIM], kg_ref[...]).astype(BF16)
    fgate_ref[...] = _silu(proj(FOURIER_WIDTH, FOURIER_WIDTH))

    f_in = proj(0, FOURIER_WIDTH).astype(BF16)
    for g in range(N_FOURIER_GROUPS):
        lo = g * FOURIER_GROUP
        z = jnp.dot(f_in[:, lo:lo + FOURIER_GROUP], cdft_ref[...], preferred_element_type=F32)
        xcs_ref[0, :, lo:lo + FOURIER_GROUP] = z[:, :FOURIER_GROUP].astype(BF16)
        xcs_ref[1, :, lo:lo + FOURIER_GROUP] = z[:, FOURIER_GROUP:].astype(BF16)
    v_ref[...] = proj(q_lo + ATTN_WIDTH + KV_WIDTH, KV_WIDTH).astype(BF16)


def _even_in(x, norm_g, w_in, q_gain, k_gain, rope, cdft):
    fixed = lambda b, i: (0, 0)
    table = pl.BlockSpec((ROW_TILE, HEAD_DIM), lambda b, i: (i, 0))
    return pl.pallas_call(
        _even_in_kernel,
        grid=(BATCH, SEQ_TILES),
        in_specs=[
            _bm_rows(D_MODEL),
            _resident((1, D_MODEL), fixed),
            _resident((D_MODEL, EVEN_IN_WIDTH), fixed),
            _resident((1, HEAD_DIM), fixed),
            _resident((1, HEAD_DIM), fixed),
            table, table, table,
            _resident((FOURIER_GROUP, 2 * FOURIER_GROUP), fixed),
        ],
        out_specs=[
            pl.BlockSpec((2, ROW_TILE, FOURIER_WIDTH), lambda b, i: (0, b * SEQ_TILES + i, 0)),
            _bm_rows(FOURIER_WIDTH),
            _bm_rows(ATTN_WIDTH),
            _bm_rows(KV_WIDTH),
            _bm_rows(KV_WIDTH),
            _bm_rows(ATTN_WIDTH),
        ],
        out_shape=[
            jax.ShapeDtypeStruct((2, TOKENS, FOURIER_WIDTH), BF16),
            jax.ShapeDtypeStruct((TOKENS, FOURIER_WIDTH), F32),
            jax.ShapeDtypeStruct((TOKENS, ATTN_WIDTH), BF16),
            jax.ShapeDtypeStruct((TOKENS, KV_WIDTH), BF16),
            jax.ShapeDtypeStruct((TOKENS, KV_WIDTH), BF16),
            jax.ShapeDtypeStruct((TOKENS, ATTN_WIDTH), F32),
        ],
        compiler_params=_params(2),
        name="even_in_proj",
    )(x, norm_g, w_in, q_gain, k_gain, *rope, cdft)


def _fourier_kernel(xcs_ref, dft_ref, fw_ref, gate_ref, o_ref):
    rhs = xcs_ref[...].reshape(2 * SEQ, FOURIER_GROUP)
    y = jnp.dot(dft_ref[...], rhs, preferred_element_type=F32)
    y = y * np.float32(1.0 / np.sqrt(SEQ * FOURIER_GROUP))
    fm = jnp.dot(y.astype(BF16), fw_ref[0], preferred_element_type=F32)
    o_ref[...] = (fm * gate_ref[...]).astype(BF16)


def _fourier_mix(xcs, seq_dft, fourier_w, fgate):
    blk = lambda b, g: (b, g)
    return pl.pallas_call(
        _fourier_kernel,
        grid=(BATCH, N_FOURIER_GROUPS),
        in_specs=[
            pl.BlockSpec((2, SEQ, FOURIER_GROUP), lambda b, g: (0, b, g)),
            _resident((SEQ, 2 * SEQ), lambda b, g: (0, 0)),
            pl.BlockSpec((1, FOURIER_GROUP, FOURIER_GROUP), lambda b, g: (g, 0, 0)),
            pl.BlockSpec((SEQ, FOURIER_GROUP), blk),
        ],
        out_specs=pl.BlockSpec((SEQ, FOURIER_GROUP), blk),
        out_shape=jax.ShapeDtypeStruct((TOKENS, FOURIER_WIDTH), BF16),
        compiler_params=_params(2),
        name="fourier_mix",
    )(xcs, seq_dft, fourier_w, fgate)


def _attention_kernel(q_ref, k_ref, v_ref, gate_ref, o_ref, vaug_ref):
    vaug_ref[:, :HEAD_DIM] = v_ref[...]
    vaug_ref[:, HEAD_DIM:] = jnp.ones((SEQ, HEAD_DIM), BF16)
    k = k_ref[...]

    def q_tile(i, carry):
        r = pl.multiple_of(i * ATTN_Q_TILE, ATTN_Q_TILE)
        rows = pl.ds(r, ATTN_Q_TILE)

        def scores(h):
            return lax.dot_general(q_ref[rows, h * HEAD_DIM:(h + 1) * HEAD_DIM], k,
                                   (((1,), (1,)), ((), ())), preferred_element_type=F32)

        s_next = scores(0)
        for h in range(Q_PER_KV):
            cols = slice(h * HEAD_DIM, (h + 1) * HEAD_DIM)
            s = s_next
            if h + 1 < Q_PER_KV:
                s_next = scores(h + 1)
            p = jnp.exp2(s - jnp.max(s, axis=-1, keepdims=True)).astype(BF16)
            ol = jnp.dot(p, vaug_ref[...], preferred_element_type=F32)
            o = ol[:, :HEAD_DIM] / ol[:, HEAD_DIM:]
            o_ref[rows, cols] = (o * gate_ref[rows, cols]).astype(BF16)
        return carry

    lax.fori_loop(0, SEQ // ATTN_Q_TILE, q_tile, 0)


def _attention(q, k, v, agate):
    group_width = Q_PER_KV * HEAD_DIM
    blk = lambda b, h: (b, h)
    return pl.pallas_call(
        _attention_kernel,
        grid=(BATCH, N_KV_HEADS),
        in_specs=[
            pl.BlockSpec((SEQ, group_width), blk),
            pl.BlockSpec((SEQ, HEAD_DIM), blk),
            pl.BlockSpec((SEQ, HEAD_DIM), blk),
            pl.BlockSpec((SEQ, group_width), blk),
        ],
        out_specs=pl.BlockSpec((SEQ, group_width), blk),
        out_shape=jax.ShapeDtypeStruct((TOKENS, ATTN_WIDTH), BF16),
        scratch_shapes=[pltpu.VMEM((SEQ, 2 * HEAD_DIM), BF16)],
        compiler_params=_params(2),
        name="gqa_attention",
    )(q, k, v, agate)


def _out_proj_kernel(*refs, n_mix, final_norm):
    x_ref = refs[0]
    mix_refs = refs[1:1 + n_mix]
    w_ref = refs[1 + n_mix]
    o_ref = refs[-1]
    acc = x_ref[...]
    lo = 0
    for m_ref in mix_refs:
        width = m_ref.shape[1]
        acc = acc + jnp.dot(m_ref[...], w_ref[lo:lo + width, :], preferred_element_type=F32)
        lo += width
    if final_norm:
        acc = _rms_norm(acc, refs[2 + n_mix][...])
    o_ref[...] = acc


def _out_proj(x, mixes, w_out, final_g=None):
    row = lambda i: (i, 0)
    fixed = lambda i: (0, 0)
    in_specs = [pl.BlockSpec((OUT_ROW_TILE, D_MODEL), row)]
    in_specs += [pl.BlockSpec((OUT_ROW_TILE, m.shape[1]), row) for m in mixes]
    in_specs += [_resident((MIX_WIDTH, D_MODEL), fixed)]
    args = [x, *mixes, w_out]
    if final_g is not None:
        in_specs += [_resident((1, D_MODEL), fixed)]
        args += [final_g]
    return pl.pallas_call(
        functools.partial(_out_proj_kernel, n_mix=len(mixes), final_norm=final_g is not None),
        grid=(TOKENS // OUT_ROW_TILE,),
        in_specs=in_specs,
        out_specs=pl.BlockSpec((OUT_ROW_TILE, D_MODEL), row),
        out_shape=jax.ShapeDtypeStruct((TOKENS, D_MODEL), F32),
        compiler_params=_params(1),
        name="out_proj",
    )(*args)


def _odd_in_kernel(x_ref, g_ref, w_ref, xr_ref, gate_ref):
    hb = _rms_norm(x_ref[...], g_ref[...]).astype(BF16)
    gate_ref[...] = _silu(jnp.dot(hb, w_ref[:, RNN_WIDTH:], preferred_element_type=F32))
    xr_ref[...] = jnp.dot(hb, w_ref[:, :RNN_WIDTH], preferred_element_type=F32)


def _odd_in(x, norm_g, w_in):
    row = lambda i: (i, 0)
    fixed = lambda i: (0, 0)
    return pl.pallas_call(
        _odd_in_kernel,
        grid=(TOKENS // ROW_TILE,),
        in_specs=[
            pl.BlockSpec((ROW_TILE, D_MODEL), row),
            _resident((1, D_MODEL), fixed),
            _resident((D_MODEL, 2 * RNN_WIDTH), fixed),
        ],
        out_specs=[pl.BlockSpec((ROW_TILE, RNN_WIDTH), row)] * 2,
        out_shape=[jax.ShapeDtypeStruct((TOKENS, RNN_WIDTH), F32)] * 2,
        compiler_params=_params(1),
        name="odd_in_proj",
    )(x, norm_g, w_in)


def _rglru_kernel(xr_ref, prev_ref, next_ref, cw_ref, cb_ref, gw_ref, gb_ref, lam_ref,
                  gate_ref, y_ref, ext_ref, xh_all_ref, hf_all_ref, h_ref):
    phase = pl.program_id(1)
    j = pl.program_id(2)
    n_rows = SCAN_CHUNK_ROWS

    @pl.when(j == 0)
    def _():
        h_ref[...] = jnp.zeros_like(h_ref)

    def chunk_scan(base, reverse, emit):
        half_log_decay = (-0.5 * RG_LRU_C) * jax.nn.softplus(-lam_ref[0])
        half_bias = 0.5 * gb_ref[0]
        order = (lambda r: reversed(r)) if reverse else (lambda r: r)
        h = h_ref[...]
        for lo in order(range(0, n_rows, GATE_ROWS)):
            xh = xh_all_ref[pl.ds(base + lo, GATE_ROWS), :]
            zh = jnp.dot(xh.astype(BF16), gw_ref[0, 0], preferred_element_type=F32) + half_bias
            t_r = jnp.tanh(zh[:, :RNN_BLOCK])
            t_i = jnp.tanh(zh[:, RNN_BLOCK:])
            log_a = half_log_decay * t_r + half_log_decay
            a = jnp.exp(log_a)
            w = jnp.tanh(log_a) * (-1.0 - a * a)
            sqrt_w = w * lax.rsqrt(jnp.maximum(w, F32_MIN_NORMAL))
            u = sqrt_w * (xh * t_i + xh)
            for r in order(range(0, GATE_ROWS, BATCH)):
                h = a[r:r + BATCH, :] * h + u[r:r + BATCH, :]
                emit(lo + r, h)
        h_ref[...] = h

    @pl.when(phase == 0)
    def _():
        base = pl.multiple_of(j * n_rows, n_rows)
        ext_ref[:HALO_PREV_ROWS, :] = jnp.where(j > 0, prev_ref[...], 0.0)
        ext_ref[HALO_PREV_ROWS:HALO_PREV_ROWS + n_rows, :] = xr_ref[...]
        ext_ref[HALO_PREV_ROWS + n_rows:, :] = jnp.where(j < N_SCAN_CHUNKS - 1, next_ref[...], 0.0)
        half_w = 0.5 * cw_ref[...]
        xh = 0.5 * cb_ref[...] + ext_ref[0:n_rows, :] * half_w[0:1, :]
        for tap in range(1, CONV_WIDTH):
            xh = xh + ext_ref[tap * BATCH:tap * BATCH + n_rows, :] * half_w[tap:tap + 1, :]
        xh_all_ref[pl.ds(base, n_rows), :] = xh

        def emit(r, h):
            hf_all_ref[pl.ds(base + r, BATCH), :] = h

        chunk_scan(base, False, emit)

    @pl.when(phase == 1)
    def _():
        base = pl.multiple_of((N_SCAN_CHUNKS - 1 - j) * n_rows, n_rows)

        def emit(r, h):
            y = (hf_all_ref[pl.ds(base + r, BATCH), :] + h) * gate_ref[r:r + BATCH, :]
            y_ref[r:r + BATCH, :] = y.astype(BF16)

        chunk_scan(base, True, emit)


def _rglru_mix(xr, gate, conv_w, conv_b, gate_w, gate_b, lam):
    n = N_SCAN_CHUNKS
    rows = SCAN_CHUNK_ROWS
    prev_per_chunk = rows // HALO_PREV_ROWS
    next_per_chunk = rows // HALO_NEXT_ROWS
    last_next = TOKENS // HALO_NEXT_ROWS - 1

    def fwd_chunk(p, j):
        return j * (1 - p) + (n - 1) * p

    def bwd_chunk(p, j):
        return (n - 1) - j * p

    return pl.pallas_call(
        _rglru_kernel,
        grid=(N_RNN_BLOCKS, 2, n),
        in_specs=[
            pl.BlockSpec((rows, RNN_BLOCK), lambda c, p, j: (fwd_chunk(p, j), c)),
            pl.BlockSpec((HALO_PREV_ROWS, RNN_BLOCK),
                         lambda c, p, j: (jnp.maximum(fwd_chunk(p, j) * prev_per_chunk - 1, 0), c)),
            pl.BlockSpec((HALO_NEXT_ROWS, RNN_BLOCK),
                         lambda c, p, j: (jnp.minimum((fwd_chunk(p, j) + 1) * next_per_chunk, last_next), c)),
            pl.BlockSpec((CONV_WIDTH, RNN_BLOCK), lambda c, p, j: (0, c)),
            pl.BlockSpec((1, RNN_BLOCK), lambda c, p, j: (0, c)),
            pl.BlockSpec((1, 1, RNN_BLOCK, 2 * RNN_BLOCK), lambda c, p, j: (p, c, 0, 0)),
            pl.BlockSpec((1, 1, 2 * RNN_BLOCK), lambda c, p, j: (p * N_RNN_BLOCKS + c, 0, 0)),
            pl.BlockSpec((1, 1, RNN_BLOCK), lambda c, p, j: (p * N_RNN_BLOCKS + c, 0, 0)),
            pl.BlockSpec((rows, RNN_BLOCK), lambda c, p, j: (bwd_chunk(p, j), c)),
        ],
        out_specs=pl.BlockSpec((rows, RNN_BLOCK), lambda c, p, j: (bwd_chunk(p, j), c)),
        out_shape=jax.ShapeDtypeStruct((TOKENS, RNN_WIDTH), BF16),
        scratch_shapes=[
            pltpu.VMEM((rows + HALO_PREV_ROWS + HALO_NEXT_ROWS, RNN_BLOCK), F32),
            pltpu.VMEM((TOKENS, RNN_BLOCK), F32),
            pltpu.VMEM((TOKENS, RNN_BLOCK), F32),
            pltpu.VMEM((BATCH, RNN_BLOCK), F32),
        ],
        compiler_params=_params(3),
        name="rglru_mix",
    )(xr, xr, xr, conv_w, conv_b, gate_w, gate_b, lam, gate)


def _dft_matrix(n, sin_sign):
    j = np.arange(n)
    ang = 2.0 * np.pi * ((j[:, None] * j[None, :]) % n) / n
    table = np.concatenate([np.cos(ang), sin_sign * np.sin(ang)], axis=1).astype(np.float32)
    return jnp.asarray(table).astype(BF16)


def _channel_dft():
    return _dft_matrix(FOURIER_GROUP, 1.0)


def _sequence_dft():
    return _dft_matrix(SEQ, -1.0)


def _rope_tables():
    rows = SEQ // GRID_W
    row = jnp.broadcast_to(jnp.arange(rows)[:, None], (rows, GRID_W)).reshape(-1).astype(F32)
    col = jnp.broadcast_to(jnp.arange(GRID_W)[None, :], (rows, GRID_W)).reshape(-1).astype(F32)
    inv_freq = ROPE_THETA ** (-jnp.arange(0, ROPE_AXIS_DIM, 2, dtype=F32) / ROPE_AXIS_DIM)
    ang_r = row[:, None] * inv_freq[None, :]
    ang_c = col[:, None] * inv_freq[None, :]
    zero = jnp.zeros_like(ang_r)
    cos = jnp.concatenate([jnp.cos(ang_r)] * 2 + [jnp.cos(ang_c)] * 2, axis=-1)
    sin_first = jnp.concatenate([-jnp.sin(ang_r), zero, -jnp.sin(ang_c), zero], axis=-1)
    sin_second = jnp.concatenate([zero, jnp.sin(ang_r), zero, jnp.sin(ang_c)], axis=-1)
    return cos, sin_first, sin_second


def _even_layer(x, norm_g, w_in, fourier_w, q_gain, k_gain, w_out, tables):
    rope, cdft, seq_dft = tables
    xcs, fgate, q, k, v, agate = _even_in(
        x, norm_g.reshape(1, D_MODEL), w_in.astype(BF16),
        q_gain.reshape(1, HEAD_DIM), k_gain.reshape(1, HEAD_DIM), rope, cdft)
    f_out = _fourier_mix(xcs, seq_dft, fourier_w.astype(BF16), fgate)
    a_out = _attention(q, k, v, agate)
    return _out_proj(x, [f_out, a_out], w_out.astype(BF16))


def _odd_layer(x, norm_g, w_in, conv_w, conv_b, w_a, b_a, w_x, b_x, lam, w_out, final_g):
    xr, gate = _odd_in(x, norm_g.reshape(1, D_MODEL), w_in.astype(BF16))
    gate_w = jnp.concatenate([w_a, w_x], axis=-1).astype(BF16)
    gate_b = jnp.concatenate([b_a, b_x], axis=-1).reshape(2 * N_RNN_BLOCKS, 1, 2 * RNN_BLOCK)
    y = _rglru_mix(xr, gate, conv_w, conv_b.reshape(1, RNN_WIDTH), gate_w, gate_b,
                   lam.reshape(2 * N_RNN_BLOCKS, 1, RNN_BLOCK))
    return _out_proj(x, [y], w_out.astype(BF16), final_g)


def kernel(x, even_norm, even_w_in, fourier_w, q_gain, k_gain, even_w_out, odd_norm, odd_w_in,
           conv_w, conv_b, gate_a_w, gate_a_b, gate_x_w, gate_x_b, rglru_lambda, odd_w_out,
           final_norm):
    tables = (_rope_tables(), _channel_dft(), _sequence_dft())
    def reorder(a, major, minor):
        return jnp.transpose(a.reshape(major, minor, D_MODEL), (1, 0, 2)).reshape(TOKENS, D_MODEL)

    h = x.reshape(TOKENS, D_MODEL)
    for layer in range(DEPTH):
        i = layer // 2
        if layer % 2 == 0:
            if layer > 0:
                h = reorder(h, SEQ, BATCH)
            h = _even_layer(h, even_norm[i], even_w_in[i], fourier_w[i], q_gain[i],
                            k_gain[i], even_w_out[i], tables)
        else:
            final_g = final_norm.reshape(1, D_MODEL) if layer == DEPTH - 1 else None
            h = _odd_layer(reorder(h, BATCH, SEQ), odd_norm[i], odd_w_in[i], conv_w[i], conv_b[i], gate_a_w[i],
                           gate_a_b[i], gate_x_w[i], gate_x_b[i], rglru_lambda[i], odd_w_out[i],
                           final_g)
    return reorder(h, SEQ, BATCH).reshape(BATCH, SEQ, D_MODEL)
```

```python
import functools

import jax
import jax.numpy as jnp
import numpy as np
from jax import lax
from jax.experimental import pallas as pl
from jax.experimental.pallas import tpu as pltpu

D_MODEL = 1024
BATCH = 16
SEQ = 2048
DEPTH = 4
TOKENS = BATCH * SEQ

GRID_W = 64
MIX_WIDTH = 2 * D_MODEL
FOURIER_WIDTH = MIX_WIDTH // 2
N_FOURIER_GROUPS = 4
FOURIER_GROUP = FOURIER_WIDTH // N_FOURIER_GROUPS
ATTN_WIDTH = MIX_WIDTH - FOURIER_WIDTH
HEAD_DIM = 128
N_Q_HEADS = ATTN_WIDTH // HEAD_DIM
N_KV_HEADS = 2
Q_PER_KV = N_Q_HEADS // N_KV_HEADS
KV_WIDTH = N_KV_HEADS * HEAD_DIM
ROPE_THETA = 10000.0
ROPE_AXIS_DIM = HEAD_DIM // 2
RNN_WIDTH = MIX_WIDTH
N_RNN_BLOCKS = 16
RNN_BLOCK = RNN_WIDTH // N_RNN_BLOCKS
CONV_WIDTH = 4
CONV_LEFT = 2
RG_LRU_C = 8.0
NORM_EPS = 1e-6
EVEN_IN_WIDTH = 2 * FOURIER_WIDTH + ATTN_WIDTH + 2 * KV_WIDTH + ATTN_WIDTH

F32 = jnp.float32
BF16 = jnp.bfloat16
F32_MIN_NORMAL = float(np.finfo(np.float32).tiny)

VMEM_LIMIT_BYTES = 56 * 1024 * 1024

ROW_TILE = 512
SEQ_TILES = SEQ // ROW_TILE
OUT_ROW_TILE = 1024
ATTN_Q_TILE = 512
SCAN_CHUNK_STEPS = 256
SCAN_CHUNK_ROWS = SCAN_CHUNK_STEPS * BATCH
N_SCAN_CHUNKS = SEQ // SCAN_CHUNK_STEPS
GATE_ROWS = 512
HALO_PREV_ROWS = CONV_LEFT * BATCH
HALO_NEXT_ROWS = (CONV_WIDTH - 1 - CONV_LEFT) * BATCH

QK_SCALE = float(HEAD_DIM ** -0.5 * np.log2(np.e))


def _resident(block_shape, index_map):
    return pl.BlockSpec(block_shape, index_map, pipeline_mode=pl.Buffered(1))


def _params(n_axes):
    return pltpu.CompilerParams(
        dimension_semantics=("arbitrary",) * n_axes,
        vmem_limit_bytes=VMEM_LIMIT_BYTES)


def _rms_norm(x, gain):
    return x * lax.rsqrt(jnp.mean(x * x, axis=-1, keepdims=True) + NORM_EPS) * gain


def _silu(x):
    return x * jax.nn.sigmoid(x)


def _bm_rows(width):
    return pl.BlockSpec((ROW_TILE, width), lambda b, i: (b * SEQ_TILES + i, 0))


def _even_in_kernel(x_ref, g_ref, w_ref, qg_ref, kg_ref, cos_ref, sina_ref, sinb_ref,
                    cdft_ref, xcs_ref, fgate_ref, q_ref, k_ref, v_ref, agate_ref):
    hb = _rms_norm(x_ref[...], g_ref[...]).astype(BF16)

    def proj(lo, width):
        return jnp.dot(hb, w_ref[:, lo:lo + width], preferred_element_type=F32)

    cos = cos_ref[...]
    sina = sina_ref[...]
    sinb = sinb_ref[...]

    def norm_rope(y, gain):
        yn = _rms_norm(y, gain)
        return (yn * cos + pltpu.roll(yn, HEAD_DIM - ROPE_AXIS_DIM // 2, 1) * sina
                + pltpu.roll(yn, ROPE_AXIS_DIM // 2, 1) * sinb)

    q_lo = 2 * FOURIER_WIDTH
    q = proj(q_lo, ATTN_WIDTH)
    for h in range(N_Q_HEADS):
        lo = h * HEAD_DIM
        qh = norm_rope(q[:, lo:lo + HEAD_DIM], qg_ref[...]) * QK_SCALE
        q_ref[:, lo:lo + HEAD_DIM] = qh.astype(BF16)
    agate_ref[...] = _silu(proj(q_lo + ATTN_WIDTH + 2 * KV_WIDTH, ATTN_WIDTH))
    k = proj(q_lo + ATTN_WIDTH, KV_WIDTH)
    for h in range(N_KV_HEADS):
        lo = h * HEAD_DIM
        k_ref[:, lo:lo + HEAD_DIM] = norm_rope(k[:, lo:lo + HEAD_DIM], kg_ref[...]).astype(BF16)
    fgate_ref[...] = _silu(proj(FOURIER_WIDTH, FOURIER_WIDTH))

    f_in = proj(0, FOURIER_WIDTH).astype(BF16)
    for g in range(N_FOURIER_GROUPS):
        lo = g * FOURIER_GROUP
        z = jnp.dot(f_in[:, lo:lo + FOURIER_GROUP], cdft_ref[...], preferred_element_type=F32)
        xcs_ref[0, :, lo:lo + FOURIER_GROUP] = z[:, :FOURIER_GROUP].astype(BF16)
        xcs_ref[1, :, lo:lo + FOURIER_GROUP] = z[:, FOURIER_GROUP:].astype(BF16)
    v_ref[...] = proj(q_lo + ATTN_WIDTH + KV_WIDTH, KV_WIDTH).astype(BF16)


def _even_in(x, norm_g, w_in, q_gain, k_gain, rope, cdft):
    fixed = lambda b, i: (0, 0)
    table = pl.BlockSpec((ROW_TILE, HEAD_DIM), lambda b, i: (i, 0))
    return pl.pallas_call(
        _even_in_kernel,
        grid=(BATCH, SEQ_TILES),
        in_specs=[
            _bm_rows(D_MODEL),
            _resident((1, D_MODEL), fixed),
            _resident((D_MODEL, EVEN_IN_WIDTH), fixed),
            _resident((1, HEAD_DIM), fixed),
            _resident((1, HEAD_DIM), fixed),
            table, table, table,
            _resident((FOURIER_GROUP, 2 * FOURIER_GROUP), fixed),
        ],
        out_specs=[
            pl.BlockSpec((2, ROW_TILE, FOURIER_WIDTH), lambda b, i: (0, b * SEQ_TILES + i, 0)),
            _bm_rows(FOURIER_WIDTH),
            _bm_rows(ATTN_WIDTH),
            _bm_rows(KV_WIDTH),
            _bm_rows(KV_WIDTH),
            _bm_rows(ATTN_WIDTH),
        ],
        out_shape=[
            jax.ShapeDtypeStruct((2, TOKENS, FOURIER_WIDTH), BF16),
            jax.ShapeDtypeStruct((TOKENS, FOURIER_WIDTH), F32),
            jax.ShapeDtypeStruct((TOKENS, ATTN_WIDTH), BF16),
            jax.ShapeDtypeStruct((TOKENS, KV_WIDTH), BF16),
            jax.ShapeDtypeStruct((TOKENS, KV_WIDTH), BF16),
            jax.ShapeDtypeStruct((TOKENS, ATTN_WIDTH), F32),
        ],
        compiler_params=_params(2),
        name="even_in_proj",
    )(x, norm_g, w_in, q_gain, k_gain, *rope, cdft)


def _fourier_kernel(xcs_ref, cos_ref, sin_ref, alt_ref, flip_ref, fw_ref, gate_ref, o_ref):
    half = SEQ // 2
    scale = np.float32(1.0 / np.sqrt(SEQ * FOURIER_GROUP))
    xc = xcs_ref[0]
    p = jnp.dot(cos_ref[...], xc, preferred_element_type=F32)
    q = jnp.dot(sin_ref[...], xcs_ref[1], preferred_element_type=F32)
    nyquist = jnp.dot(alt_ref[...], xc, preferred_element_type=F32)[0:1, :]
    y_lo = ((p - q) * scale).astype(BF16)
    mirrored = jnp.dot(flip_ref[...], ((p + q) * scale).astype(BF16), preferred_element_type=F32)
    row = lax.broadcasted_iota(jnp.int32, (half, FOURIER_GROUP), 0)
    y_hi = jnp.where(row == 0, nyquist * scale, mirrored).astype(BF16)
    fw = fw_ref[0]
    o_ref[:half, :] = (jnp.dot(y_lo, fw, preferred_element_type=F32) * gate_ref[:half, :]).astype(BF16)
    o_ref[half:, :] = (jnp.dot(y_hi, fw, preferred_element_type=F32) * gate_ref[half:, :]).astype(BF16)


def _fourier_mix(xcs, seq_dft, fourier_w, fgate):
    blk = lambda b, g: (b, g)
    fixed = lambda b, g: (0, 0)
    return pl.pallas_call(
        _fourier_kernel,
        grid=(BATCH, N_FOURIER_GROUPS),
        in_specs=[
            pl.BlockSpec((2, SEQ, FOURIER_GROUP), lambda b, g: (0, b, g)),
            _resident((SEQ // 2, SEQ), fixed),
            _resident((SEQ // 2, SEQ), fixed),
            _resident((8, SEQ), fixed),
            _resident((SEQ // 2, SEQ // 2), fixed),
            pl.BlockSpec((1, FOURIER_GROUP, FOURIER_GROUP), lambda b, g: (g, 0, 0)),
            pl.BlockSpec((SEQ, FOURIER_GROUP), blk),
        ],
        out_specs=pl.BlockSpec((SEQ, FOURIER_GROUP), blk),
        out_shape=jax.ShapeDtypeStruct((TOKENS, FOURIER_WIDTH), BF16),
        compiler_params=_params(2),
        name="fourier_mix",
    )(xcs, *seq_dft, fourier_w, fgate)


def _attention_kernel(q_ref, k_ref, v_ref, gate_ref, o_ref, vaug_ref):
    vaug_ref[:, :HEAD_DIM] = v_ref[...]
    vaug_ref[:, HEAD_DIM:] = jnp.ones((SEQ, HEAD_DIM), BF16)
    k = k_ref[...]

    def q_tile(i, carry):
        r = pl.multiple_of(i * ATTN_Q_TILE, ATTN_Q_TILE)
        rows = pl.ds(r, ATTN_Q_TILE)

        def scores(h):
            return lax.dot_general(q_ref[rows, h * HEAD_DIM:(h + 1) * HEAD_DIM], k,
                                   (((1,), (1,)), ((), ())), preferred_element_type=F32)

        s_next = scores(0)
        for h in range(Q_PER_KV):
            cols = slice(h * HEAD_DIM, (h + 1) * HEAD_DIM)
            s = s_next
            if h + 1 < Q_PER_KV:
                s_next = scores(h + 1)
            p = jnp.exp2(s - jnp.max(s, axis=-1, keepdims=True)).astype(BF16)
            ol = jnp.dot(p, vaug_ref[...], preferred_element_type=F32)
            o = ol[:, :HEAD_DIM] / ol[:, HEAD_DIM:]
            o_ref[rows, cols] = (o * gate_ref[rows, cols]).astype(BF16)
        return carry

    lax.fori_loop(0, SEQ // ATTN_Q_TILE, q_tile, 0)


def _attention(q, k, v, agate):
    group_width = Q_PER_KV * HEAD_DIM
    blk = lambda b, h: (b, h)
    return pl.pallas_call(
        _attention_kernel,
        grid=(BATCH, N_KV_HEADS),
        in_specs=[
            pl.BlockSpec((SEQ, group_width), blk),
            pl.BlockSpec((SEQ, HEAD_DIM), blk),
            pl.BlockSpec((SEQ, HEAD_DIM), blk),
            pl.BlockSpec((SEQ, group_width), blk),
        ],
        out_specs=pl.BlockSpec((SEQ, group_width), blk),
        out_shape=jax.ShapeDtypeStruct((TOKENS, ATTN_WIDTH), BF16),
        scratch_shapes=[pltpu.VMEM((SEQ, 2 * HEAD_DIM), BF16)],
        compiler_params=_params(2),
        name="gqa_attention",
    )(q, k, v, agate)


def _out_proj_kernel(*refs, n_mix, final_norm):
    x_ref = refs[0]
    mix_refs = refs[1:1 + n_mix]
    w_ref = refs[1 + n_mix]
    o_ref = refs[-1]
    acc = x_ref[...]
    lo = 0
    for m_ref in mix_refs:
        width = m_ref.shape[1]
        acc = acc + jnp.dot(m_ref[...], w_ref[lo:lo + width, :], preferred_element_type=F32)
        lo += width
    if final_norm:
        acc = _rms_norm(acc, refs[2 + n_mix][...])
    o_ref[...] = acc


def _out_proj(x, mixes, w_out, final_g=None):
    row = lambda i: (i, 0)
    fixed = lambda i: (0, 0)
    in_specs = [pl.BlockSpec((OUT_ROW_TILE, D_MODEL), row)]
    in_specs += [pl.BlockSpec((OUT_ROW_TILE, m.shape[1]), row) for m in mixes]
    in_specs += [_resident((MIX_WIDTH, D_MODEL), fixed)]
    args = [x, *mixes, w_out]
    if final_g is not None:
        in_specs += [_resident((1, D_MODEL), fixed)]
        args += [final_g]
    return pl.pallas_call(
        functools.partial(_out_proj_kernel, n_mix=len(mixes), final_norm=final_g is not None),
        grid=(TOKENS // OUT_ROW_TILE,),
        in_specs=in_specs,
        out_specs=pl.BlockSpec((OUT_ROW_TILE, D_MODEL), row),
        out_shape=jax.ShapeDtypeStruct((TOKENS, D_MODEL), F32),
        compiler_params=_params(1),
        name="out_proj",
    )(*args)


def _odd_in_kernel(x_ref, g_ref, w_ref, xr_ref, gate_ref):
    hb = _rms_norm(x_ref[...], g_ref[...]).astype(BF16)
    gate_ref[...] = _silu(jnp.dot(hb, w_ref[:, RNN_WIDTH:], preferred_element_type=F32))
    xr_ref[...] = jnp.dot(hb, w_ref[:, :RNN_WIDTH], preferred_element_type=F32)


def _odd_in(x, norm_g, w_in):
    row = lambda i: (i, 0)
    fixed = lambda i: (0, 0)
    return pl.pallas_call(
        _odd_in_kernel,
        grid=(TOKENS // ROW_TILE,),
        in_specs=[
            pl.BlockSpec((ROW_TILE, D_MODEL), row),
            _resident((1, D_MODEL), fixed),
            _resident((D_MODEL, 2 * RNN_WIDTH), fixed),
        ],
        out_specs=[pl.BlockSpec((ROW_TILE, RNN_WIDTH), row)] * 2,
        out_shape=[jax.ShapeDtypeStruct((TOKENS, RNN_WIDTH), F32)] * 2,
        compiler_params=_params(1),
        name="odd_in_proj",
    )(x, norm_g, w_in)


def _rglru_kernel(xr_ref, prev_ref, next_ref, cw_ref, cb_ref, gw_ref, gb_ref, lam_ref,
                  gate_ref, y_ref, ext_ref, xh_all_ref, hf_all_ref, h_ref):
    phase = pl.program_id(1)
    j = pl.program_id(2)
    n_rows = SCAN_CHUNK_ROWS

    @pl.when(j == 0)
    def _():
        h_ref[...] = jnp.zeros_like(h_ref)

    def chunk_scan(base, reverse, emit):
        half_log_decay = (-0.5 * RG_LRU_C) * jax.nn.softplus(-lam_ref[0])
        half_bias = 0.5 * gb_ref[0]
        order = (lambda r: reversed(r)) if reverse else (lambda r: r)
        h = h_ref[...]
        for lo in order(range(0, n_rows, GATE_ROWS)):
            xh = xh_all_ref[pl.ds(base + lo, GATE_ROWS), :]
            zh = jnp.dot(xh.astype(BF16), gw_ref[0, 0], preferred_element_type=F32) + half_bias
            t_r = jnp.tanh(zh[:, :RNN_BLOCK])
            t_i = jnp.tanh(zh[:, RNN_BLOCK:])
            log_a = half_log_decay * t_r + half_log_decay
            a = jnp.exp(log_a)
            w = jnp.tanh(log_a) * (-1.0 - a * a)
            sqrt_w = w * lax.rsqrt(jnp.maximum(w, F32_MIN_NORMAL))
            u = sqrt_w * (xh * t_i + xh)
            for r in order(range(0, GATE_ROWS, BATCH)):
                h = a[r:r + BATCH, :] * h + u[r:r + BATCH, :]
                emit(lo + r, h)
        h_ref[...] = h

    @pl.when(phase == 0)
    def _():
        base = pl.multiple_of(j * n_rows, n_rows)
        ext_ref[:HALO_PREV_ROWS, :] = jnp.where(j > 0, prev_ref[...], 0.0)
        ext_ref[HALO_PREV_ROWS:HALO_PREV_ROWS + n_rows, :] = xr_ref[...]
        ext_ref[HALO_PREV_ROWS + n_rows:, :] = jnp.where(j < N_SCAN_CHUNKS - 1, next_ref[...], 0.0)
        half_w = 0.5 * cw_ref[...]
        xh = 0.5 * cb_ref[...] + ext_ref[0:n_rows, :] * half_w[0:1, :]
        for tap in range(1, CONV_WIDTH):
            xh = xh + ext_ref[tap * BATCH:tap * BATCH + n_rows, :] * half_w[tap:tap + 1, :]
        xh_all_ref[pl.ds(base, n_rows), :] = xh

        def emit(r, h):
            hf_all_ref[pl.ds(base + r, BATCH), :] = h

        chunk_scan(base, False, emit)

    @pl.when(phase == 1)
    def _():
        base = pl.multiple_of((N_SCAN_CHUNKS - 1 - j) * n_rows, n_rows)

        def emit(r, h):
            y = (hf_all_ref[pl.ds(base + r, BATCH), :] + h) * gate_ref[r:r + BATCH, :]
            y_ref[r:r + BATCH, :] = y.astype(BF16)

        chunk_scan(base, True, emit)


def _rglru_mix(xr, gate, conv_w, conv_b, gate_w, gate_b, lam):
    n = N_SCAN_CHUNKS
    rows = SCAN_CHUNK_ROWS
    prev_per_chunk = rows // HALO_PREV_ROWS
    next_per_chunk = rows // HALO_NEXT_ROWS
    last_next = TOKENS // HALO_NEXT_ROWS - 1

    def fwd_chunk(p, j):
        return j * (1 - p) + (n - 1) * p

    def bwd_chunk(p, j):
        return (n - 1) - j * p

    return pl.pallas_call(
        _rglru_kernel,
        grid=(N_RNN_BLOCKS, 2, n),
        in_specs=[
            pl.BlockSpec((rows, RNN_BLOCK), lambda c, p, j: (fwd_chunk(p, j), c)),
            pl.BlockSpec((HALO_PREV_ROWS, RNN_BLOCK),
                         lambda c, p, j: (jnp.maximum(fwd_chunk(p, j) * prev_per_chunk - 1, 0), c)),
            pl.BlockSpec((HALO_NEXT_ROWS, RNN_BLOCK),
                         lambda c, p, j: (jnp.minimum((fwd_chunk(p, j) + 1) * next_per_chunk, last_next), c)),
            pl.BlockSpec((CONV_WIDTH, RNN_BLOCK), lambda c, p, j: (0, c)),
            pl.BlockSpec((1, RNN_BLOCK), lambda c, p, j: (0, c)),
            pl.BlockSpec((1, 1, RNN_BLOCK, 2 * RNN_BLOCK), lambda c, p, j: (p, c, 0, 0)),
            pl.BlockSpec((1, 1, 2 * RNN_BLOCK), lambda c, p, j: (p * N_RNN_BLOCKS + c, 0, 0)),
            pl.BlockSpec((1, 1, RNN_BLOCK), lambda c, p, j: (p * N_RNN_BLOCKS + c, 0, 0)),
            pl.BlockSpec((rows, RNN_BLOCK), lambda c, p, j: (bwd_chunk(p, j), c)),
        ],
        out_specs=pl.BlockSpec((rows, RNN_BLOCK), lambda c, p, j: (bwd_chunk(p, j), c)),
        out_shape=jax.ShapeDtypeStruct((TOKENS, RNN_WIDTH), BF16),
        scratch_shapes=[
            pltpu.VMEM((rows + HALO_PREV_ROWS + HALO_NEXT_ROWS, RNN_BLOCK), F32),
            pltpu.VMEM((TOKENS, RNN_BLOCK), F32),
            pltpu.VMEM((TOKENS, RNN_BLOCK), F32),
            pltpu.VMEM((BATCH, RNN_BLOCK), F32),
        ],
        compiler_params=_params(3),
        name="rglru_mix",
    )(xr, xr, xr, conv_w, conv_b, gate_w, gate_b, lam, gate)


def _dft_angles(n_out, n):
    j = np.arange(n)
    return 2.0 * np.pi * ((np.arange(n_out)[:, None] * j[None, :]) % n) / n


def _as_operand(table):
    return jnp.asarray(table.astype(np.float32)).astype(BF16)


def _channel_dft():
    ang = _dft_angles(FOURIER_GROUP, FOURIER_GROUP)
    return _as_operand(np.concatenate([np.cos(ang), np.sin(ang)], axis=1))


def _sequence_dft():
    half = SEQ // 2
    ang = _dft_angles(half, SEQ)
    alternating = np.zeros((8, SEQ))
    alternating[0] = 1.0 - 2.0 * (np.arange(SEQ) % 2)
    flip = np.zeros((half, half))
    i = np.arange(1, half)
    flip[i, half - i] = 1.0
    return tuple(_as_operand(t) for t in (np.cos(ang), np.sin(ang), alternating, flip))


def _rope_tables():
    rows = SEQ // GRID_W
    row = jnp.broadcast_to(jnp.arange(rows)[:, None], (rows, GRID_W)).reshape(-1).astype(F32)
    col = jnp.broadcast_to(jnp.arange(GRID_W)[None, :], (rows, GRID_W)).reshape(-1).astype(F32)
    inv_freq = ROPE_THETA ** (-jnp.arange(0, ROPE_AXIS_DIM, 2, dtype=F32) / ROPE_AXIS_DIM)
    ang_r = row[:, None] * inv_freq[None, :]
    ang_c = col[:, None] * inv_freq[None, :]
    zero = jnp.zeros_like(ang_r)
    cos = jnp.concatenate([jnp.cos(ang_r)] * 2 + [jnp.cos(ang_c)] * 2, axis=-1)
    sin_first = jnp.concatenate([-jnp.sin(ang_r), zero, -jnp.sin(ang_c), zero], axis=-1)
    sin_second = jnp.concatenate([zero, jnp.sin(ang_r), zero, jnp.sin(ang_c)], axis=-1)
    return cos, sin_first, sin_second


def _even_layer(x, norm_g, w_in, fourier_w, q_gain, k_gain, w_out, tables):
    rope, cdft, seq_dft = tables
    xcs, fgate, q, k, v, agate = _even_in(
        x, norm_g.reshape(1, D_MODEL), w_in.astype(BF16),
        q_gain.reshape(1, HEAD_DIM), k_gain.reshape(1, HEAD_DIM), rope, cdft)
    f_out = _fourier_mix(xcs, seq_dft, fourier_w.astype(BF16), fgate)
    a_out = _attention(q, k, v, agate)
    return _out_proj(x, [f_out, a_out], w_out.astype(BF16))


def _odd_layer(x, norm_g, w_in, conv_w, conv_b, w_a, b_a, w_x, b_x, lam, w_out, final_g):
    xr, gate = _odd_in(x, norm_g.reshape(1, D_MODEL), w_in.astype(BF16))
    gate_w = jnp.concatenate([w_a, w_x], axis=-1).astype(BF16)
    gate_b = jnp.concatenate([b_a, b_x], axis=-1).reshape(2 * N_RNN_BLOCKS, 1, 2 * RNN_BLOCK)
    y = _rglru_mix(xr, gate, conv_w, conv_b.reshape(1, RNN_WIDTH), gate_w, gate_b,
                   lam.reshape(2 * N_RNN_BLOCKS, 1, RNN_BLOCK))
    return _out_proj(x, [y], w_out.astype(BF16), final_g)


def kernel(x, even_norm, even_w_in, fourier_w, q_gain, k_gain, even_w_out, odd_norm, odd_w_in,
           conv_w, conv_b, gate_a_w, gate_a_b, gate_x_w, gate_x_b, rglru_lambda, odd_w_out,
           final_norm):
    tables = (_rope_tables(), _channel_dft(), _sequence_dft())
    def reorder(a, major, minor):
        return jnp.transpose(a.reshape(major, minor, D_MODEL), (1, 0, 2)).reshape(TOKENS, D_MODEL)

    h = x.reshape(TOKENS, D_MODEL)
    for layer in range(DEPTH):
        i = layer // 2
        if layer % 2 == 0:
            if layer > 0:
                h = reorder(h, SEQ, BATCH)
            h = _even_layer(h, even_norm[i], even_w_in[i], fourier_w[i], q_gain[i],
                            k_gain[i], even_w_out[i], tables)
        else:
            final_g = final_norm.reshape(1, D_MODEL) if layer == DEPTH - 1 else None
            h = _odd_layer(reorder(h, BATCH, SEQ), odd_norm[i], odd_w_in[i], conv_w[i], conv_b[i], gate_a_w[i],
                           gate_a_b[i], gate_x_w[i], gate_x_b[i], rglru_lambda[i], odd_w_out[i],
                           final_g)
    return reorder(h, SEQ, BATCH).reshape(BATCH, SEQ, D_MODEL)
```

```python
import functools

import jax
import jax.numpy as jnp
import numpy as np
from jax import lax
from jax.experimental import pallas as pl
from jax.experimental.pallas import tpu as pltpu

D_MODEL = 1024
BATCH = 16
SEQ = 2048
DEPTH = 4
TOKENS = BATCH * SEQ

GRID_W = 64
MIX_WIDTH = 2 * D_MODEL
FOURIER_WIDTH = MIX_WIDTH // 2
N_FOURIER_GROUPS = 4
FOURIER_GROUP = FOURIER_WIDTH // N_FOURIER_GROUPS
ATTN_WIDTH = MIX_WIDTH - FOURIER_WIDTH
HEAD_DIM = 128
N_Q_HEADS = ATTN_WIDTH // HEAD_DIM
N_KV_HEADS = 2
Q_PER_KV = N_Q_HEADS // N_KV_HEADS
KV_WIDTH = N_KV_HEADS * HEAD_DIM
ROPE_THETA = 10000.0
ROPE_AXIS_DIM = HEAD_DIM // 2
RNN_WIDTH = MIX_WIDTH
N_RNN_BLOCKS = 16
RNN_BLOCK = RNN_WIDTH // N_RNN_BLOCKS
CONV_WIDTH = 4
CONV_LEFT = 2
RG_LRU_C = 8.0
NORM_EPS = 1e-6
EVEN_IN_WIDTH = 2 * FOURIER_WIDTH + ATTN_WIDTH + 2 * KV_WIDTH + ATTN_WIDTH

F32 = jnp.float32
BF16 = jnp.bfloat16
F32_MIN_NORMAL = float(np.finfo(np.float32).tiny)

VMEM_LIMIT_BYTES = 56 * 1024 * 1024

ROW_TILE = 512
SEQ_TILES = SEQ // ROW_TILE
OUT_ROW_TILE = 1024
ATTN_Q_TILE = 512
SCAN_CHUNK_STEPS = 256
SCAN_CHUNK_ROWS = SCAN_CHUNK_STEPS * BATCH
N_SCAN_CHUNKS = SEQ // SCAN_CHUNK_STEPS
GATE_ROWS = 512
HALO_PREV_ROWS = CONV_LEFT * BATCH
HALO_NEXT_ROWS = (CONV_WIDTH - 1 - CONV_LEFT) * BATCH

QK_SCALE = float(HEAD_DIM ** -0.5 * np.log2(np.e))


def _resident(block_shape, index_map):
    return pl.BlockSpec(block_shape, index_map, pipeline_mode=pl.Buffered(1))


def _params(n_axes):
    return pltpu.CompilerParams(
        dimension_semantics=("arbitrary",) * n_axes,
        vmem_limit_bytes=VMEM_LIMIT_BYTES)


def _rms_norm(x, gain):
    return x * lax.rsqrt(jnp.mean(x * x, axis=-1, keepdims=True) + NORM_EPS) * gain


def _silu(x):
    return x * jax.nn.sigmoid(x)


def _bm_rows(width):
    return pl.BlockSpec((ROW_TILE, width), lambda b, i: (b * SEQ_TILES + i, 0))


def _even_in_kernel(x_ref, g_ref, w_ref, qg_ref, kg_ref, cos_ref, sina_ref, sinb_ref,
                    cdft_ref, xcs_ref, fgate_ref, q_ref, k_ref, v_ref, agate_ref):
    hb = _rms_norm(x_ref[...], g_ref[...]).astype(BF16)

    def proj(lo, width):
        return jnp.dot(hb, w_ref[:, lo:lo + width], preferred_element_type=F32)

    cos = cos_ref[...]
    sina = sina_ref[...]
    sinb = sinb_ref[...]

    def norm_rope(y, gain):
        yn = _rms_norm(y, gain)
        return (yn * cos + pltpu.roll(yn, HEAD_DIM - ROPE_AXIS_DIM // 2, 1) * sina
                + pltpu.roll(yn, ROPE_AXIS_DIM // 2, 1) * sinb)

    q_lo = 2 * FOURIER_WIDTH
    q = proj(q_lo, ATTN_WIDTH)
    for h in range(N_Q_HEADS):
        lo = h * HEAD_DIM
        qh = norm_rope(q[:, lo:lo + HEAD_DIM], qg_ref[...]) * QK_SCALE
        q_ref[:, lo:lo + HEAD_DIM] = qh.astype(BF16)
    agate_ref[...] = _silu(proj(q_lo + ATTN_WIDTH + 2 * KV_WIDTH, ATTN_WIDTH))
    k = proj(q_lo + ATTN_WIDTH, KV_WIDTH)
    for h in range(N_KV_HEADS):
        lo = h * HEAD_DIM
        k_ref[:, lo:lo + HEAD_DIM] = norm_rope(k[:, lo:lo + HEAD_DIM], kg_ref[...]).astype(BF16)
    fgate_ref[...] = _silu(proj(FOURIER_WIDTH, FOURIER_WIDTH))

    f_in = proj(0, FOURIER_WIDTH).astype(BF16)
    for g in range(N_FOURIER_GROUPS):
        lo = g * FOURIER_GROUP
        z = jnp.dot(f_in[:, lo:lo + FOURIER_GROUP], cdft_ref[...], preferred_element_type=F32)
        xcs_ref[0, :, lo:lo + FOURIER_GROUP] = z[:, :FOURIER_GROUP].astype(BF16)
        xcs_ref[1, :, lo:lo + FOURIER_GROUP] = z[:, FOURIER_GROUP:].astype(BF16)
    v_ref[...] = proj(q_lo + ATTN_WIDTH + KV_WIDTH, KV_WIDTH).astype(BF16)


def _even_in(x, norm_g, w_in, q_gain, k_gain, rope, cdft):
    fixed = lambda b, i: (0, 0)
    table = pl.BlockSpec((ROW_TILE, HEAD_DIM), lambda b, i: (i, 0))
    return pl.pallas_call(
        _even_in_kernel,
        grid=(BATCH, SEQ_TILES),
        in_specs=[
            _bm_rows(D_MODEL),
            _resident((1, D_MODEL), fixed),
            _resident((D_MODEL, EVEN_IN_WIDTH), fixed),
            _resident((1, HEAD_DIM), fixed),
            _resident((1, HEAD_DIM), fixed),
            table, table, table,
            _resident((FOURIER_GROUP, 2 * FOURIER_GROUP), fixed),
        ],
        out_specs=[
            pl.BlockSpec((2, ROW_TILE, FOURIER_WIDTH), lambda b, i: (0, b * SEQ_TILES + i, 0)),
            _bm_rows(FOURIER_WIDTH),
            _bm_rows(ATTN_WIDTH),
            _bm_rows(KV_WIDTH),
            _bm_rows(KV_WIDTH),
            _bm_rows(ATTN_WIDTH),
        ],
        out_shape=[
            jax.ShapeDtypeStruct((2, TOKENS, FOURIER_WIDTH), BF16),
            jax.ShapeDtypeStruct((TOKENS, FOURIER_WIDTH), F32),
            jax.ShapeDtypeStruct((TOKENS, ATTN_WIDTH), BF16),
            jax.ShapeDtypeStruct((TOKENS, KV_WIDTH), BF16),
            jax.ShapeDtypeStruct((TOKENS, KV_WIDTH), BF16),
            jax.ShapeDtypeStruct((TOKENS, ATTN_WIDTH), F32),
        ],
        compiler_params=_params(2),
        name="even_in_proj",
    )(x, norm_g, w_in, q_gain, k_gain, *rope, cdft)


def _fourier_kernel(xcs_ref, cos_ref, sin_ref, alt_ref, flip_ref, fw_ref, gate_ref, o_ref):
    half = SEQ // 2
    scale = np.float32(1.0 / np.sqrt(SEQ * FOURIER_GROUP))
    xc = xcs_ref[0]
    p = jnp.dot(cos_ref[...], xc, preferred_element_type=F32)
    q = jnp.dot(sin_ref[...], xcs_ref[1], preferred_element_type=F32)
    nyquist = jnp.dot(alt_ref[...], xc, preferred_element_type=F32)[0:1, :]
    y_lo = ((p - q) * scale).astype(BF16)
    mirrored = jnp.dot(flip_ref[...], ((p + q) * scale).astype(BF16), preferred_element_type=F32)
    row = lax.broadcasted_iota(jnp.int32, (half, FOURIER_GROUP), 0)
    y_hi = jnp.where(row == 0, nyquist * scale, mirrored).astype(BF16)
    fw = fw_ref[0]
    o_ref[:half, :] = (jnp.dot(y_lo, fw, preferred_element_type=F32) * gate_ref[:half, :]).astype(BF16)
    o_ref[half:, :] = (jnp.dot(y_hi, fw, preferred_element_type=F32) * gate_ref[half:, :]).astype(BF16)


def _fourier_mix(xcs, seq_dft, fourier_w, fgate):
    blk = lambda b, g: (b, g)
    fixed = lambda b, g: (0, 0)
    return pl.pallas_call(
        _fourier_kernel,
        grid=(BATCH, N_FOURIER_GROUPS),
        in_specs=[
            pl.BlockSpec((2, SEQ, FOURIER_GROUP), lambda b, g: (0, b, g)),
            _resident((SEQ // 2, SEQ), fixed),
            _resident((SEQ // 2, SEQ), fixed),
            _resident((8, SEQ), fixed),
            _resident((SEQ // 2, SEQ // 2), fixed),
            pl.BlockSpec((1, FOURIER_GROUP, FOURIER_GROUP), lambda b, g: (g, 0, 0)),
            pl.BlockSpec((SEQ, FOURIER_GROUP), blk),
        ],
        out_specs=pl.BlockSpec((SEQ, FOURIER_GROUP), blk),
        out_shape=jax.ShapeDtypeStruct((TOKENS, FOURIER_WIDTH), BF16),
        compiler_params=_params(2),
        name="fourier_mix",
    )(xcs, *seq_dft, fourier_w, fgate)


def _attention_kernel(q_ref, k_ref, v_ref, gate_ref, o_ref, vaug_ref):
    vaug_ref[:, :HEAD_DIM] = v_ref[...]
    vaug_ref[:, HEAD_DIM:] = jnp.ones((SEQ, HEAD_DIM), BF16)
    k = k_ref[...]

    items = [(slice(i * ATTN_Q_TILE, (i + 1) * ATTN_Q_TILE), slice(h * HEAD_DIM, (h + 1) * HEAD_DIM))
             for i in range(SEQ // ATTN_Q_TILE) for h in range(Q_PER_KV)]

    def scores(item):
        return lax.dot_general(q_ref[item], k, (((1,), (1,)), ((), ())),
                               preferred_element_type=F32)

    s_next = scores(items[0])
    for n, item in enumerate(items):
        s = s_next
        if n + 1 < len(items):
            s_next = scores(items[n + 1])
        p = jnp.exp2(s - jnp.max(s, axis=-1, keepdims=True)).astype(BF16)
        ol = jnp.dot(p, vaug_ref[...], preferred_element_type=F32)
        o = ol[:, :HEAD_DIM] / ol[:, HEAD_DIM:]
        o_ref[item] = (o * gate_ref[item]).astype(BF16)


def _attention(q, k, v, agate):
    group_width = Q_PER_KV * HEAD_DIM
    blk = lambda b, h: (b, h)
    return pl.pallas_call(
        _attention_kernel,
        grid=(BATCH, N_KV_HEADS),
        in_specs=[
            pl.BlockSpec((SEQ, group_width), blk),
            pl.BlockSpec((SEQ, HEAD_DIM), blk),
            pl.BlockSpec((SEQ, HEAD_DIM), blk),
            pl.BlockSpec((SEQ, group_width), blk),
        ],
        out_specs=pl.BlockSpec((SEQ, group_width), blk),
        out_shape=jax.ShapeDtypeStruct((TOKENS, ATTN_WIDTH), BF16),
        scratch_shapes=[pltpu.VMEM((SEQ, 2 * HEAD_DIM), BF16)],
        compiler_params=_params(2),
        name="gqa_attention",
    )(q, k, v, agate)


def _out_proj_kernel(*refs, n_mix, final_norm):
    x_ref = refs[0]
    mix_refs = refs[1:1 + n_mix]
    w_ref = refs[-1]
    o_ref = refs[-2]

    @pl.when(pl.program_id(0) == 0)
    def _():
        w_ref[...] = refs[1 + n_mix][...].astype(BF16)

    acc = x_ref[...]
    lo = 0
    for m_ref in mix_refs:
        width = m_ref.shape[1]
        acc = acc + jnp.dot(m_ref[...], w_ref[lo:lo + width, :], preferred_element_type=F32)
        lo += width
    if final_norm:
        acc = _rms_norm(acc, refs[2 + n_mix][...])
    o_ref[...] = acc


def _out_proj(x, mixes, w_out, final_g=None):
    row = lambda i: (i, 0)
    fixed = lambda i: (0, 0)
    in_specs = [pl.BlockSpec((OUT_ROW_TILE, D_MODEL), row)]
    in_specs += [pl.BlockSpec((OUT_ROW_TILE, m.shape[1]), row) for m in mixes]
    in_specs += [_resident((MIX_WIDTH, D_MODEL), fixed)]
    args = [x, *mixes, w_out]
    if final_g is not None:
        in_specs += [_resident((1, D_MODEL), fixed)]
        args += [final_g]
    return pl.pallas_call(
        functools.partial(_out_proj_kernel, n_mix=len(mixes), final_norm=final_g is not None),
        grid=(TOKENS // OUT_ROW_TILE,),
        in_specs=in_specs,
        out_specs=pl.BlockSpec((OUT_ROW_TILE, D_MODEL), row),
        out_shape=jax.ShapeDtypeStruct((TOKENS, D_MODEL), F32),
        scratch_shapes=[pltpu.VMEM((MIX_WIDTH, D_MODEL), BF16)],
        compiler_params=_params(1),
        name="out_proj",
    )(*args)


def _odd_in_kernel(x_ref, g_ref, w32_ref, xr_ref, gate_ref, w_ref):
    @pl.when(pl.program_id(0) == 0)
    def _():
        w_ref[...] = w32_ref[...].astype(BF16)

    hb = _rms_norm(x_ref[...], g_ref[...]).astype(BF16)
    gate_ref[...] = _silu(jnp.dot(hb, w_ref[:, RNN_WIDTH:], preferred_element_type=F32))
    xr_ref[...] = jnp.dot(hb, w_ref[:, :RNN_WIDTH], preferred_element_type=F32)


def _odd_in(x, norm_g, w_in):
    row = lambda i: (i, 0)
    fixed = lambda i: (0, 0)
    return pl.pallas_call(
        _odd_in_kernel,
        grid=(TOKENS // ROW_TILE,),
        in_specs=[
            pl.BlockSpec((ROW_TILE, D_MODEL), row),
            _resident((1, D_MODEL), fixed),
            _resident((D_MODEL, 2 * RNN_WIDTH), fixed),
        ],
        out_specs=[pl.BlockSpec((ROW_TILE, RNN_WIDTH), row)] * 2,
        out_shape=[jax.ShapeDtypeStruct((TOKENS, RNN_WIDTH), F32)] * 2,
        scratch_shapes=[pltpu.VMEM((D_MODEL, 2 * RNN_WIDTH), BF16)],
        compiler_params=_params(1),
        name="odd_in_proj",
    )(x, norm_g, w_in)


def _rglru_kernel(xr_ref, prev_ref, next_ref, cw_ref, cb_ref, gw_ref, gb_ref, lam_ref,
                  gate_ref, y_ref, ext_ref, xh_all_ref, hf_all_ref, h_ref):
    phase = pl.program_id(1)
    j = pl.program_id(2)
    n_rows = SCAN_CHUNK_ROWS

    @pl.when(j == 0)
    def _():
        h_ref[...] = jnp.zeros_like(h_ref)

    def chunk_scan(base, reverse, emit):
        half_log_decay = (-0.5 * RG_LRU_C) * jax.nn.softplus(-lam_ref[0])
        half_bias = 0.5 * gb_ref[0]
        order = (lambda r: reversed(r)) if reverse else (lambda r: r)
        h = h_ref[...]
        for lo in order(range(0, n_rows, GATE_ROWS)):
            xh = xh_all_ref[pl.ds(base + lo, GATE_ROWS), :]
            zh = jnp.dot(xh.astype(BF16), gw_ref[0, 0], preferred_element_type=F32) + half_bias
            t_r = jnp.tanh(zh[:, :RNN_BLOCK])
            t_i = jnp.tanh(zh[:, RNN_BLOCK:])
            log_a = half_log_decay * t_r + half_log_decay
            a = jnp.exp(log_a)
            w = jnp.tanh(log_a) * (-1.0 - a * a)
            sqrt_w = w * lax.rsqrt(jnp.maximum(w, F32_MIN_NORMAL))
            u = sqrt_w * (xh * t_i + xh)
            for r in order(range(0, GATE_ROWS, BATCH)):
                h = a[r:r + BATCH, :] * h + u[r:r + BATCH, :]
                emit(lo + r, h)
        h_ref[...] = h

    @pl.when(phase == 0)
    def _():
        base = pl.multiple_of(j * n_rows, n_rows)
        ext_ref[:HALO_PREV_ROWS, :] = jnp.where(j > 0, prev_ref[...], 0.0)
        ext_ref[HALO_PREV_ROWS:HALO_PREV_ROWS + n_rows, :] = xr_ref[...]
        ext_ref[HALO_PREV_ROWS + n_rows:, :] = jnp.where(j < N_SCAN_CHUNKS - 1, next_ref[...], 0.0)
        half_w = 0.5 * cw_ref[...]
        xh = 0.5 * cb_ref[...] + ext_ref[0:n_rows, :] * half_w[0:1, :]
        for tap in range(1, CONV_WIDTH):
            xh = xh + ext_ref[tap * BATCH:tap * BATCH + n_rows, :] * half_w[tap:tap + 1, :]
        xh_all_ref[pl.ds(base, n_rows), :] = xh

        def emit(r, h):
            hf_all_ref[pl.ds(base + r, BATCH), :] = h

        chunk_scan(base, False, emit)

    @pl.when(phase == 1)
    def _():
        base = pl.multiple_of((N_SCAN_CHUNKS - 1 - j) * n_rows, n_rows)

        def emit(r, h):
            y = (hf_all_ref[pl.ds(base + r, BATCH), :] + h) * gate_ref[r:r + BATCH, :]
            y_ref[r:r + BATCH, :] = y.astype(BF16)

        chunk_scan(base, True, emit)


def _rglru_mix(xr, gate, conv_w, conv_b, gate_w, gate_b, lam):
    n = N_SCAN_CHUNKS
    rows = SCAN_CHUNK_ROWS
    prev_per_chunk = rows // HALO_PREV_ROWS
    next_per_chunk = rows // HALO_NEXT_ROWS
    last_next = TOKENS // HALO_NEXT_ROWS - 1

    def fwd_chunk(p, j):
        return j * (1 - p) + (n - 1) * p

    def bwd_chunk(p, j):
        return (n - 1) - j * p

    return pl.pallas_call(
        _rglru_kernel,
        grid=(N_RNN_BLOCKS, 2, n),
        in_specs=[
            pl.BlockSpec((rows, RNN_BLOCK), lambda c, p, j: (fwd_chunk(p, j), c)),
            pl.BlockSpec((HALO_PREV_ROWS, RNN_BLOCK),
                         lambda c, p, j: (jnp.maximum(fwd_chunk(p, j) * prev_per_chunk - 1, 0), c)),
            pl.BlockSpec((HALO_NEXT_ROWS, RNN_BLOCK),
                         lambda c, p, j: (jnp.minimum((fwd_chunk(p, j) + 1) * next_per_chunk, last_next), c)),
            pl.BlockSpec((CONV_WIDTH, RNN_BLOCK), lambda c, p, j: (0, c)),
            pl.BlockSpec((1, RNN_BLOCK), lambda c, p, j: (0, c)),
            pl.BlockSpec((1, 1, RNN_BLOCK, 2 * RNN_BLOCK), lambda c, p, j: (p, c, 0, 0)),
            pl.BlockSpec((1, 1, 2 * RNN_BLOCK), lambda c, p, j: (p * N_RNN_BLOCKS + c, 0, 0)),
            pl.BlockSpec((1, 1, RNN_BLOCK), lambda c, p, j: (p * N_RNN_BLOCKS + c, 0, 0)),
            pl.BlockSpec((rows, RNN_BLOCK), lambda c, p, j: (bwd_chunk(p, j), c)),
        ],
        out_specs=pl.BlockSpec((rows, RNN_BLOCK), lambda c, p, j: (bwd_chunk(p, j), c)),
        out_shape=jax.ShapeDtypeStruct((TOKENS, RNN_WIDTH), BF16),
        scratch_shapes=[
            pltpu.VMEM((rows + HALO_PREV_ROWS + HALO_NEXT_ROWS, RNN_BLOCK), F32),
            pltpu.VMEM((TOKENS, RNN_BLOCK), F32),
            pltpu.VMEM((TOKENS, RNN_BLOCK), F32),
            pltpu.VMEM((BATCH, RNN_BLOCK), F32),
        ],
        compiler_params=_params(3),
        name="rglru_mix",
    )(xr, xr, xr, conv_w, conv_b, gate_w, gate_b, lam, gate)


def _dft_angles(n_out, n):
    j = np.arange(n)
    return 2.0 * np.pi * ((np.arange(n_out)[:, None] * j[None, :]) % n) / n


def _as_operand(table):
    return jnp.asarray(table.astype(np.float32)).astype(BF16)


def _channel_dft():
    ang = _dft_angles(FOURIER_GROUP, FOURIER_GROUP)
    return _as_operand(np.concatenate([np.cos(ang), np.sin(ang)], axis=1))


def _sequence_dft():
    half = SEQ // 2
    ang = _dft_angles(half, SEQ)
    alternating = np.zeros((8, SEQ))
    alternating[0] = 1.0 - 2.0 * (np.arange(SEQ) % 2)
    flip = np.zeros((half, half))
    i = np.arange(1, half)
    flip[i, half - i] = 1.0
    return tuple(_as_operand(t) for t in (np.cos(ang), np.sin(ang), alternating, flip))


def _rope_tables():
    rows = SEQ // GRID_W
    row = jnp.broadcast_to(jnp.arange(rows)[:, None], (rows, GRID_W)).reshape(-1).astype(F32)
    col = jnp.broadcast_to(jnp.arange(GRID_W)[None, :], (rows, GRID_W)).reshape(-1).astype(F32)
    inv_freq = ROPE_THETA ** (-jnp.arange(0, ROPE_AXIS_DIM, 2, dtype=F32) / ROPE_AXIS_DIM)
    ang_r = row[:, None] * inv_freq[None, :]
    ang_c = col[:, None] * inv_freq[None, :]
    zero = jnp.zeros_like(ang_r)
    cos = jnp.concatenate([jnp.cos(ang_r)] * 2 + [jnp.cos(ang_c)] * 2, axis=-1)
    sin_first = jnp.concatenate([-jnp.sin(ang_r), zero, -jnp.sin(ang_c), zero], axis=-1)
    sin_second = jnp.concatenate([zero, jnp.sin(ang_r), zero, jnp.sin(ang_c)], axis=-1)
    return cos, sin_first, sin_second


def _even_layer(x, norm_g, w_in, fourier_w, q_gain, k_gain, w_out, tables):
    rope, cdft, seq_dft = tables
    xcs, fgate, q, k, v, agate = _even_in(
        x, norm_g.reshape(1, D_MODEL), w_in.astype(BF16),
        q_gain.reshape(1, HEAD_DIM), k_gain.reshape(1, HEAD_DIM), rope, cdft)
    f_out = _fourier_mix(xcs, seq_dft, fourier_w.astype(BF16), fgate)
    a_out = _attention(q, k, v, agate)
    return _out_proj(x, [f_out, a_out], w_out)


def _odd_layer(x, norm_g, w_in, conv_w, conv_b, w_a, b_a, w_x, b_x, lam, w_out, final_g):
    xr, gate = _odd_in(x, norm_g.reshape(1, D_MODEL), w_in)
    gate_w = jnp.concatenate([w_a, w_x], axis=-1).astype(BF16)
    gate_b = jnp.concatenate([b_a, b_x], axis=-1).reshape(2 * N_RNN_BLOCKS, 1, 2 * RNN_BLOCK)
    y = _rglru_mix(xr, gate, conv_w, conv_b.reshape(1, RNN_WIDTH), gate_w, gate_b,
                   lam.reshape(2 * N_RNN_BLOCKS, 1, RNN_BLOCK))
    return _out_proj(x, [y], w_out, final_g)


def kernel(x, even_norm, even_w_in, fourier_w, q_gain, k_gain, even_w_out, odd_norm, odd_w_in,
           conv_w, conv_b, gate_a_w, gate_a_b, gate_x_w, gate_x_b, rglru_lambda, odd_w_out,
           final_norm):
    tables = (_rope_tables(), _channel_dft(), _sequence_dft())
    def reorder(a, major, minor):
        return jnp.transpose(a.reshape(major, minor, D_MODEL), (1, 0, 2)).reshape(TOKENS, D_MODEL)

    h = x.reshape(TOKENS, D_MODEL)
    for layer in range(DEPTH):
        i = layer // 2
        if layer % 2 == 0:
            if layer > 0:
                h = reorder(h, SEQ, BATCH)
            h = _even_layer(h, even_norm[i], even_w_in[i], fourier_w[i], q_gain[i],
                            k_gain[i], even_w_out[i], tables)
        else:
            final_g = final_norm.reshape(1, D_MODEL) if layer == DEPTH - 1 else None
            h = _odd_layer(reorder(h, BATCH, SEQ), odd_norm[i], odd_w_in[i], conv_w[i], conv_b[i], gate_a_w[i],
                           gate_a_b[i], gate_x_w[i], gate_x_b[i], rglru_lambda[i], odd_w_out[i],
                           final_g)
    return reorder(h, SEQ, BATCH).reshape(BATCH, SEQ, D_MODEL)
```

```python
import functools

import jax
import jax.numpy as jnp
import numpy as np
from jax import lax
from jax.experimental import pallas as pl
from jax.experimental.pallas import tpu as pltpu

D_MODEL = 1024
BATCH = 16
SEQ = 2048
DEPTH = 4
TOKENS = BATCH * SEQ

GRID_W = 64
MIX_WIDTH = 2 * D_MODEL
FOURIER_WIDTH = MIX_WIDTH // 2
N_FOURIER_GROUPS = 4
FOURIER_GROUP = FOURIER_WIDTH // N_FOURIER_GROUPS
ATTN_WIDTH = MIX_WIDTH - FOURIER_WIDTH
HEAD_DIM = 128
N_Q_HEADS = ATTN_WIDTH // HEAD_DIM
N_KV_HEADS = 2
Q_PER_KV = N_Q_HEADS // N_KV_HEADS
KV_WIDTH = N_KV_HEADS * HEAD_DIM
ROPE_THETA = 10000.0
ROPE_AXIS_DIM = HEAD_DIM // 2
RNN_WIDTH = MIX_WIDTH
N_RNN_BLOCKS = 16
RNN_BLOCK = RNN_WIDTH // N_RNN_BLOCKS
CONV_WIDTH = 4
CONV_LEFT = 2
RG_LRU_C = 8.0
NORM_EPS = 1e-6
EVEN_IN_WIDTH = 2 * FOURIER_WIDTH + ATTN_WIDTH + 2 * KV_WIDTH + ATTN_WIDTH

F32 = jnp.float32
BF16 = jnp.bfloat16
F32_MIN_NORMAL = float(np.finfo(np.float32).tiny)

VMEM_LIMIT_BYTES = 56 * 1024 * 1024

ROW_TILE = 512
SEQ_TILES = SEQ // ROW_TILE
OUT_ROW_TILE = 1024
ATTN_Q_TILE = 512
SCAN_CHUNK_STEPS = 256
SCAN_CHUNK_ROWS = SCAN_CHUNK_STEPS * BATCH
N_SCAN_CHUNKS = SEQ // SCAN_CHUNK_STEPS
GATE_ROWS = 512
HALO_PREV_ROWS = CONV_LEFT * BATCH
HALO_NEXT_ROWS = (CONV_WIDTH - 1 - CONV_LEFT) * BATCH

QK_SCALE = float(HEAD_DIM ** -0.5 * np.log2(np.e))


def _resident(block_shape, index_map):
    return pl.BlockSpec(block_shape, index_map, pipeline_mode=pl.Buffered(1))


def _params(n_axes):
    return pltpu.CompilerParams(
        dimension_semantics=("arbitrary",) * n_axes,
        vmem_limit_bytes=VMEM_LIMIT_BYTES)


def _rms_norm(x, gain):
    return x * lax.rsqrt(jnp.mean(x * x, axis=-1, keepdims=True) + NORM_EPS) * gain


def _silu(x):
    return x * jax.nn.sigmoid(x)


def _bm_rows(width):
    return pl.BlockSpec((ROW_TILE, width), lambda b, i: (b * SEQ_TILES + i, 0))


def _even_in_kernel(x_ref, g_ref, w_ref, qg_ref, kg_ref, cos_ref, sina_ref, sinb_ref,
                    cdft_ref, xcs_ref, fgate_ref, q_ref, k_ref, v_ref, agate_ref):
    hb = _rms_norm(x_ref[...], g_ref[...]).astype(BF16)

    def proj(lo, width):
        return jnp.dot(hb, w_ref[:, lo:lo + width], preferred_element_type=F32)

    cos = cos_ref[...]
    sina = sina_ref[...]
    sinb = sinb_ref[...]

    def norm_rope(y, gain):
        yn = _rms_norm(y, gain)
        return (yn * cos + pltpu.roll(yn, HEAD_DIM - ROPE_AXIS_DIM // 2, 1) * sina
                + pltpu.roll(yn, ROPE_AXIS_DIM // 2, 1) * sinb)

    q_lo = 2 * FOURIER_WIDTH
    q = proj(q_lo, ATTN_WIDTH)
    for h in range(N_Q_HEADS):
        lo = h * HEAD_DIM
        qh = norm_rope(q[:, lo:lo + HEAD_DIM], qg_ref[...]) * QK_SCALE
        q_ref[:, lo:lo + HEAD_DIM] = qh.astype(BF16)
    agate_ref[...] = _silu(proj(q_lo + ATTN_WIDTH + 2 * KV_WIDTH, ATTN_WIDTH))
    k = proj(q_lo + ATTN_WIDTH, KV_WIDTH)
    for h in range(N_KV_HEADS):
        lo = h * HEAD_DIM
        k_ref[:, lo:lo + HEAD_DIM] = norm_rope(k[:, lo:lo + HEAD_DIM], kg_ref[...]).astype(BF16)
    fgate_ref[...] = _silu(proj(FOURIER_WIDTH, FOURIER_WIDTH))

    f_in = proj(0, FOURIER_WIDTH).astype(BF16)
    for g in range(N_FOURIER_GROUPS):
        lo = g * FOURIER_GROUP
        z = jnp.dot(f_in[:, lo:lo + FOURIER_GROUP], cdft_ref[...], preferred_element_type=F32)
        xcs_ref[0, :, lo:lo + FOURIER_GROUP] = z[:, :FOURIER_GROUP].astype(BF16)
        xcs_ref[1, :, lo:lo + FOURIER_GROUP] = z[:, FOURIER_GROUP:].astype(BF16)
    v_ref[...] = proj(q_lo + ATTN_WIDTH + KV_WIDTH, KV_WIDTH).astype(BF16)


def _even_in(x, norm_g, w_in, q_gain, k_gain, rope, cdft):
    fixed = lambda b, i: (0, 0)
    table = pl.BlockSpec((ROW_TILE, HEAD_DIM), lambda b, i: (i, 0))
    return pl.pallas_call(
        _even_in_kernel,
        grid=(BATCH, SEQ_TILES),
        in_specs=[
            _bm_rows(D_MODEL),
            _resident((1, D_MODEL), fixed),
            _resident((D_MODEL, EVEN_IN_WIDTH), fixed),
            _resident((1, HEAD_DIM), fixed),
            _resident((1, HEAD_DIM), fixed),
            table, table, table,
            _resident((FOURIER_GROUP, 2 * FOURIER_GROUP), fixed),
        ],
        out_specs=[
            pl.BlockSpec((2, ROW_TILE, FOURIER_WIDTH), lambda b, i: (0, b * SEQ_TILES + i, 0)),
            _bm_rows(FOURIER_WIDTH),
            _bm_rows(ATTN_WIDTH),
            _bm_rows(KV_WIDTH),
            _bm_rows(KV_WIDTH),
            _bm_rows(ATTN_WIDTH),
        ],
        out_shape=[
            jax.ShapeDtypeStruct((2, TOKENS, FOURIER_WIDTH), BF16),
            jax.ShapeDtypeStruct((TOKENS, FOURIER_WIDTH), F32),
            jax.ShapeDtypeStruct((TOKENS, ATTN_WIDTH), BF16),
            jax.ShapeDtypeStruct((TOKENS, KV_WIDTH), BF16),
            jax.ShapeDtypeStruct((TOKENS, KV_WIDTH), BF16),
            jax.ShapeDtypeStruct((TOKENS, ATTN_WIDTH), F32),
        ],
        compiler_params=_params(2),
        name="even_in_proj",
    )(x, norm_g, w_in, q_gain, k_gain, *rope, cdft)


def _fourier_kernel(xcs_ref, cos_ref, sin_ref, alt_ref, flip_ref, fw_ref, gate_ref, o_ref):
    half = SEQ // 2
    scale = np.float32(1.0 / np.sqrt(SEQ * FOURIER_GROUP))
    xc = xcs_ref[0]
    p = jnp.dot(cos_ref[...], xc, preferred_element_type=F32)
    q = jnp.dot(sin_ref[...], xcs_ref[1], preferred_element_type=F32)
    nyquist = jnp.dot(alt_ref[...], xc, preferred_element_type=F32)[0:1, :]
    y_lo = ((p - q) * scale).astype(BF16)
    mirrored = jnp.dot(flip_ref[...], ((p + q) * scale).astype(BF16), preferred_element_type=F32)
    row = lax.broadcasted_iota(jnp.int32, (half, FOURIER_GROUP), 0)
    y_hi = jnp.where(row == 0, nyquist * scale, mirrored).astype(BF16)
    fw = fw_ref[0]
    o_ref[:half, :] = (jnp.dot(y_lo, fw, preferred_element_type=F32) * gate_ref[:half, :]).astype(BF16)
    o_ref[half:, :] = (jnp.dot(y_hi, fw, preferred_element_type=F32) * gate_ref[half:, :]).astype(BF16)


def _fourier_mix(xcs, seq_dft, fourier_w, fgate):
    blk = lambda b, g: (b, g)
    fixed = lambda b, g: (0, 0)
    return pl.pallas_call(
        _fourier_kernel,
        grid=(BATCH, N_FOURIER_GROUPS),
        in_specs=[
            pl.BlockSpec((2, SEQ, FOURIER_GROUP), lambda b, g: (0, b, g)),
            _resident((SEQ // 2, SEQ), fixed),
            _resident((SEQ // 2, SEQ), fixed),
            _resident((8, SEQ), fixed),
            _resident((SEQ // 2, SEQ // 2), fixed),
            pl.BlockSpec((1, FOURIER_GROUP, FOURIER_GROUP), lambda b, g: (g, 0, 0)),
            pl.BlockSpec((SEQ, FOURIER_GROUP), blk),
        ],
        out_specs=pl.BlockSpec((SEQ, FOURIER_GROUP), blk),
        out_shape=jax.ShapeDtypeStruct((TOKENS, FOURIER_WIDTH), BF16),
        compiler_params=_params(2),
        name="fourier_mix",
    )(xcs, *seq_dft, fourier_w, fgate)


def _attention_kernel(q_ref, k_ref, v_ref, gate_ref, o_ref, vaug_ref):
    vaug_ref[:, :HEAD_DIM] = v_ref[...]
    vaug_ref[:, HEAD_DIM:] = jnp.ones((SEQ, HEAD_DIM), BF16)
    k = k_ref[...]

    items = [(slice(i * ATTN_Q_TILE, (i + 1) * ATTN_Q_TILE), slice(h * HEAD_DIM, (h + 1) * HEAD_DIM))
             for i in range(SEQ // ATTN_Q_TILE) for h in range(Q_PER_KV)]

    def scores(item):
        return lax.dot_general(q_ref[item], k, (((1,), (1,)), ((), ())),
                               preferred_element_type=F32)

    s_next = scores(items[0])
    for n, item in enumerate(items):
        s = s_next
        if n + 1 < len(items):
            s_next = scores(items[n + 1])
        p = jnp.exp2(s - jnp.max(s, axis=-1, keepdims=True)).astype(BF16)
        ol = jnp.dot(p, vaug_ref[...], preferred_element_type=F32)
        o = ol[:, :HEAD_DIM] / ol[:, HEAD_DIM:]
        o_ref[item] = (o * gate_ref[item]).astype(BF16)


def _attention(q, k, v, agate):
    group_width = Q_PER_KV * HEAD_DIM
    blk = lambda b, h: (b, h)
    return pl.pallas_call(
        _attention_kernel,
        grid=(BATCH, N_KV_HEADS),
        in_specs=[
            pl.BlockSpec((SEQ, group_width), blk),
            pl.BlockSpec((SEQ, HEAD_DIM), blk),
            pl.BlockSpec((SEQ, HEAD_DIM), blk),
            pl.BlockSpec((SEQ, group_width), blk),
        ],
        out_specs=pl.BlockSpec((SEQ, group_width), blk),
        out_shape=jax.ShapeDtypeStruct((TOKENS, ATTN_WIDTH), BF16),
        scratch_shapes=[pltpu.VMEM((SEQ, 2 * HEAD_DIM), BF16)],
        compiler_params=_params(2),
        name="gqa_attention",
    )(q, k, v, agate)


def _out_proj_kernel(*refs, n_mix, final_norm):
    x_ref = refs[0]
    mix_refs = refs[1:1 + n_mix]
    w_ref = refs[-1]
    o_ref = refs[-2]

    @pl.when(pl.program_id(0) == 0)
    def _():
        w_ref[...] = refs[1 + n_mix][...].astype(BF16)

    acc = x_ref[...]
    lo = 0
    for m_ref in mix_refs:
        width = m_ref.shape[1]
        acc = acc + jnp.dot(m_ref[...], w_ref[lo:lo + width, :], preferred_element_type=F32)
        lo += width
    if final_norm:
        acc = _rms_norm(acc, refs[2 + n_mix][...])
    o_ref[...] = acc


def _out_proj(x, mixes, w_out, layer, final_g=None):
    row = lambda i: (i, 0)
    fixed = lambda i: (0, 0)
    in_specs = [pl.BlockSpec((OUT_ROW_TILE, D_MODEL), row)]
    in_specs += [pl.BlockSpec((OUT_ROW_TILE, m.shape[1]), row) for m in mixes]
    in_specs += [_resident((None, MIX_WIDTH, D_MODEL), lambda i: (layer, 0, 0))]
    args = [x, *mixes, w_out]
    if final_g is not None:
        in_specs += [_resident((1, D_MODEL), fixed)]
        args += [final_g]
    return pl.pallas_call(
        functools.partial(_out_proj_kernel, n_mix=len(mixes), final_norm=final_g is not None),
        grid=(TOKENS // OUT_ROW_TILE,),
        in_specs=in_specs,
        out_specs=pl.BlockSpec((OUT_ROW_TILE, D_MODEL), row),
        out_shape=jax.ShapeDtypeStruct((TOKENS, D_MODEL), F32),
        scratch_shapes=[pltpu.VMEM((MIX_WIDTH, D_MODEL), BF16)],
        compiler_params=_params(1),
        name="out_proj",
    )(*args)


def _odd_in_kernel(x_ref, g_ref, w32_ref, xr_ref, gate_ref, w_ref):
    @pl.when(pl.program_id(0) == 0)
    def _():
        w_ref[...] = w32_ref[...].astype(BF16)

    hb = _rms_norm(x_ref[...], g_ref[...]).astype(BF16)
    gate_ref[...] = _silu(jnp.dot(hb, w_ref[:, RNN_WIDTH:], preferred_element_type=F32))
    xr_ref[...] = jnp.dot(hb, w_ref[:, :RNN_WIDTH], preferred_element_type=F32)


def _odd_in(x, norm_g, w_in, layer):
    row = lambda i: (i, 0)
    fixed = lambda i: (0, 0)
    return pl.pallas_call(
        _odd_in_kernel,
        grid=(TOKENS // ROW_TILE,),
        in_specs=[
            pl.BlockSpec((ROW_TILE, D_MODEL), row),
            _resident((1, D_MODEL), fixed),
            _resident((None, D_MODEL, 2 * RNN_WIDTH), lambda i: (layer, 0, 0)),
        ],
        out_specs=[pl.BlockSpec((ROW_TILE, RNN_WIDTH), row)] * 2,
        out_shape=[jax.ShapeDtypeStruct((TOKENS, RNN_WIDTH), F32)] * 2,
        scratch_shapes=[pltpu.VMEM((D_MODEL, 2 * RNN_WIDTH), BF16)],
        compiler_params=_params(1),
        name="odd_in_proj",
    )(x, norm_g, w_in)


def _rglru_kernel(xr_ref, prev_ref, next_ref, cw_ref, cb_ref, gw_ref, gb_ref, lam_ref,
                  gate_ref, y_ref, ext_ref, xh_all_ref, hf_all_ref, h_ref):
    phase = pl.program_id(1)
    j = pl.program_id(2)
    n_rows = SCAN_CHUNK_ROWS

    @pl.when(j == 0)
    def _():
        h_ref[...] = jnp.zeros_like(h_ref)

    def chunk_scan(base, reverse, emit):
        half_log_decay = (-0.5 * RG_LRU_C) * jax.nn.softplus(-lam_ref[0])
        half_bias = 0.5 * gb_ref[0]
        order = (lambda r: reversed(r)) if reverse else (lambda r: r)
        h = h_ref[...]
        for lo in order(range(0, n_rows, GATE_ROWS)):
            xh = xh_all_ref[pl.ds(base + lo, GATE_ROWS), :]
            zh = jnp.dot(xh.astype(BF16), gw_ref[0, 0], preferred_element_type=F32) + half_bias
            t_r = jnp.tanh(zh[:, :RNN_BLOCK])
            t_i = jnp.tanh(zh[:, RNN_BLOCK:])
            log_a = half_log_decay * t_r + half_log_decay
            a = jnp.exp(log_a)
            w = jnp.tanh(log_a) * (-1.0 - a * a)
            sqrt_w = w * lax.rsqrt(jnp.maximum(w, F32_MIN_NORMAL))
            u = sqrt_w * (xh * t_i + xh)
            for r in order(range(0, GATE_ROWS, BATCH)):
                h = a[r:r + BATCH, :] * h + u[r:r + BATCH, :]
                emit(lo + r, h)
        h_ref[...] = h

    @pl.when(phase == 0)
    def _():
        base = pl.multiple_of(j * n_rows, n_rows)
        ext_ref[:HALO_PREV_ROWS, :] = jnp.where(j > 0, prev_ref[...], 0.0)
        ext_ref[HALO_PREV_ROWS:HALO_PREV_ROWS + n_rows, :] = xr_ref[...]
        ext_ref[HALO_PREV_ROWS + n_rows:, :] = jnp.where(j < N_SCAN_CHUNKS - 1, next_ref[...], 0.0)
        half_w = 0.5 * cw_ref[...]
        xh = 0.5 * cb_ref[...] + ext_ref[0:n_rows, :] * half_w[0:1, :]
        for tap in range(1, CONV_WIDTH):
            xh = xh + ext_ref[tap * BATCH:tap * BATCH + n_rows, :] * half_w[tap:tap + 1, :]
        xh_all_ref[pl.ds(base, n_rows), :] = xh

        def emit(r, h):
            hf_all_ref[pl.ds(base + r, BATCH), :] = h

        chunk_scan(base, False, emit)

    @pl.when(phase == 1)
    def _():
        base = pl.multiple_of((N_SCAN_CHUNKS - 1 - j) * n_rows, n_rows)

        def emit(r, h):
            y = (hf_all_ref[pl.ds(base + r, BATCH), :] + h) * gate_ref[r:r + BATCH, :]
            y_ref[r:r + BATCH, :] = y.astype(BF16)

        chunk_scan(base, True, emit)


def _rglru_mix(xr, gate, conv_w, conv_b, gate_w, gate_b, lam):
    n = N_SCAN_CHUNKS
    rows = SCAN_CHUNK_ROWS
    prev_per_chunk = rows // HALO_PREV_ROWS
    next_per_chunk = rows // HALO_NEXT_ROWS
    last_next = TOKENS // HALO_NEXT_ROWS - 1

    def fwd_chunk(p, j):
        return j * (1 - p) + (n - 1) * p

    def bwd_chunk(p, j):
        return (n - 1) - j * p

    return pl.pallas_call(
        _rglru_kernel,
        grid=(N_RNN_BLOCKS, 2, n),
        in_specs=[
            pl.BlockSpec((rows, RNN_BLOCK), lambda c, p, j: (fwd_chunk(p, j), c)),
            pl.BlockSpec((HALO_PREV_ROWS, RNN_BLOCK),
                         lambda c, p, j: (jnp.maximum(fwd_chunk(p, j) * prev_per_chunk - 1, 0), c)),
            pl.BlockSpec((HALO_NEXT_ROWS, RNN_BLOCK),
                         lambda c, p, j: (jnp.minimum((fwd_chunk(p, j) + 1) * next_per_chunk, last_next), c)),
            pl.BlockSpec((CONV_WIDTH, RNN_BLOCK), lambda c, p, j: (0, c)),
            pl.BlockSpec((1, RNN_BLOCK), lambda c, p, j: (0, c)),
            pl.BlockSpec((1, 1, RNN_BLOCK, 2 * RNN_BLOCK), lambda c, p, j: (p, c, 0, 0)),
            pl.BlockSpec((1, 1, 2 * RNN_BLOCK), lambda c, p, j: (p * N_RNN_BLOCKS + c, 0, 0)),
            pl.BlockSpec((1, 1, RNN_BLOCK), lambda c, p, j: (p * N_RNN_BLOCKS + c, 0, 0)),
            pl.BlockSpec((rows, RNN_BLOCK), lambda c, p, j: (bwd_chunk(p, j), c)),
        ],
        out_specs=pl.BlockSpec((rows, RNN_BLOCK), lambda c, p, j: (bwd_chunk(p, j), c)),
        out_shape=jax.ShapeDtypeStruct((TOKENS, RNN_WIDTH), BF16),
        scratch_shapes=[
            pltpu.VMEM((rows + HALO_PREV_ROWS + HALO_NEXT_ROWS, RNN_BLOCK), F32),
            pltpu.VMEM((TOKENS, RNN_BLOCK), F32),
            pltpu.VMEM((TOKENS, RNN_BLOCK), F32),
            pltpu.VMEM((BATCH, RNN_BLOCK), F32),
        ],
        compiler_params=_params(3),
        name="rglru_mix",
    )(xr, xr, xr, conv_w, conv_b, gate_w, gate_b, lam, gate)


def _dft_angles(n_out, n):
    j = np.arange(n)
    return 2.0 * np.pi * ((np.arange(n_out)[:, None] * j[None, :]) % n) / n


def _as_operand(table):
    return jnp.asarray(table.astype(np.float32)).astype(BF16)


def _channel_dft():
    ang = _dft_angles(FOURIER_GROUP, FOURIER_GROUP)
    return _as_operand(np.concatenate([np.cos(ang), np.sin(ang)], axis=1))


def _sequence_dft():
    half = SEQ // 2
    ang = _dft_angles(half, SEQ)
    alternating = np.zeros((8, SEQ))
    alternating[0] = 1.0 - 2.0 * (np.arange(SEQ) % 2)
    flip = np.zeros((half, half))
    i = np.arange(1, half)
    flip[i, half - i] = 1.0
    return tuple(_as_operand(t) for t in (np.cos(ang), np.sin(ang), alternating, flip))


def _rope_tables():
    rows = SEQ // GRID_W
    row = jnp.broadcast_to(jnp.arange(rows)[:, None], (rows, GRID_W)).reshape(-1).astype(F32)
    col = jnp.broadcast_to(jnp.arange(GRID_W)[None, :], (rows, GRID_W)).reshape(-1).astype(F32)
    inv_freq = ROPE_THETA ** (-jnp.arange(0, ROPE_AXIS_DIM, 2, dtype=F32) / ROPE_AXIS_DIM)
    ang_r = row[:, None] * inv_freq[None, :]
    ang_c = col[:, None] * inv_freq[None, :]
    zero = jnp.zeros_like(ang_r)
    cos = jnp.concatenate([jnp.cos(ang_r)] * 2 + [jnp.cos(ang_c)] * 2, axis=-1)
    sin_first = jnp.concatenate([-jnp.sin(ang_r), zero, -jnp.sin(ang_c), zero], axis=-1)
    sin_second = jnp.concatenate([zero, jnp.sin(ang_r), zero, jnp.sin(ang_c)], axis=-1)
    return cos, sin_first, sin_second


def _even_layer(x, norm_g, w_in, fourier_w, q_gain, k_gain, w_out, tables):
    rope, cdft, seq_dft = tables
    xcs, fgate, q, k, v, agate = _even_in(
        x, norm_g.reshape(1, D_MODEL), w_in.astype(BF16),
        q_gain.reshape(1, HEAD_DIM), k_gain.reshape(1, HEAD_DIM), rope, cdft)
    f_out = _fourier_mix(xcs, seq_dft, fourier_w.astype(BF16), fgate)
    a_out = _attention(q, k, v, agate)
    return _out_proj(x, [f_out, a_out], *w_out)


def _odd_layer(x, norm_g, w_in, conv_w, conv_b, w_a, b_a, w_x, b_x, lam, w_out, final_g):
    xr, gate = _odd_in(x, norm_g.reshape(1, D_MODEL), *w_in)
    gate_w = jnp.concatenate([w_a, w_x], axis=-1).astype(BF16)
    gate_b = jnp.concatenate([b_a, b_x], axis=-1).reshape(2 * N_RNN_BLOCKS, 1, 2 * RNN_BLOCK)
    y = _rglru_mix(xr, gate, conv_w, conv_b.reshape(1, RNN_WIDTH), gate_w, gate_b,
                   lam.reshape(2 * N_RNN_BLOCKS, 1, RNN_BLOCK))
    return _out_proj(x, [y], *w_out, final_g)


def kernel(x, even_norm, even_w_in, fourier_w, q_gain, k_gain, even_w_out, odd_norm, odd_w_in,
           conv_w, conv_b, gate_a_w, gate_a_b, gate_x_w, gate_x_b, rglru_lambda, odd_w_out,
           final_norm):
    tables = (_rope_tables(), _channel_dft(), _sequence_dft())
    def reorder(a, major, minor):
        return jnp.transpose(a.reshape(major, minor, D_MODEL), (1, 0, 2)).reshape(TOKENS, D_MODEL)

    h = x.reshape(TOKENS, D_MODEL)
    for layer in range(DEPTH):
        i = layer // 2
        if layer % 2 == 0:
            if layer > 0:
                h = reorder(h, SEQ, BATCH)
            h = _even_layer(h, even_norm[i], even_w_in[i], fourier_w[i], q_gain[i],
                            k_gain[i], (even_w_out, i), tables)
        else:
            final_g = final_norm.reshape(1, D_MODEL) if layer == DEPTH - 1 else None
            h = _odd_layer(reorder(h, BATCH, SEQ), odd_norm[i], (odd_w_in, i), conv_w[i], conv_b[i],
                           gate_a_w[i], gate_a_b[i], gate_x_w[i], gate_x_b[i], rglru_lambda[i],
                           (odd_w_out, i), final_g)
    return reorder(h, SEQ, BATCH).reshape(BATCH, SEQ, D_MODEL)
```

```python
import functools

import jax
import jax.numpy as jnp
import numpy as np
from jax import lax
from jax.experimental import pallas as pl
from jax.experimental.pallas import tpu as pltpu

D_MODEL = 1024
BATCH = 16
SEQ = 2048
DEPTH = 4
TOKENS = BATCH * SEQ

GRID_W = 64
MIX_WIDTH = 2 * D_MODEL
FOURIER_WIDTH = MIX_WIDTH // 2
N_FOURIER_GROUPS = 4
FOURIER_GROUP = FOURIER_WIDTH // N_FOURIER_GROUPS
ATTN_WIDTH = MIX_WIDTH - FOURIER_WIDTH
HEAD_DIM = 128
N_Q_HEADS = ATTN_WIDTH // HEAD_DIM
N_KV_HEADS = 2
Q_PER_KV = N_Q_HEADS // N_KV_HEADS
KV_WIDTH = N_KV_HEADS * HEAD_DIM
ROPE_THETA = 10000.0
ROPE_AXIS_DIM = HEAD_DIM // 2
RNN_WIDTH = MIX_WIDTH
N_RNN_BLOCKS = 16
RNN_BLOCK = RNN_WIDTH // N_RNN_BLOCKS
CONV_WIDTH = 4
CONV_LEFT = 2
RG_LRU_C = 8.0
NORM_EPS = 1e-6
EVEN_IN_WIDTH = 2 * FOURIER_WIDTH + ATTN_WIDTH + 2 * KV_WIDTH + ATTN_WIDTH

F32 = jnp.float32
BF16 = jnp.bfloat16
F32_MIN_NORMAL = float(np.finfo(np.float32).tiny)

VMEM_LIMIT_BYTES = 56 * 1024 * 1024

ROW_TILE = 512
SEQ_TILES = SEQ // ROW_TILE
OUT_ROW_TILE = 1024
ATTN_Q_TILE = 512
SCAN_CHUNK_STEPS = 256
SCAN_CHUNK_ROWS = SCAN_CHUNK_STEPS * BATCH
N_SCAN_CHUNKS = SEQ // SCAN_CHUNK_STEPS
GATE_ROWS = 512
HALO_PREV_ROWS = CONV_LEFT * BATCH
HALO_NEXT_ROWS = (CONV_WIDTH - 1 - CONV_LEFT) * BATCH

QK_SCALE = float(HEAD_DIM ** -0.5 * np.log2(np.e))


def _resident(block_shape, index_map):
    return pl.BlockSpec(block_shape, index_map, pipeline_mode=pl.Buffered(1))


def _params(n_axes):
    return pltpu.CompilerParams(
        dimension_semantics=("arbitrary",) * n_axes,
        vmem_limit_bytes=VMEM_LIMIT_BYTES)


def _rms_norm(x, gain):
    return x * lax.rsqrt(jnp.mean(x * x, axis=-1, keepdims=True) + NORM_EPS) * gain


def _silu(x):
    return x * jax.nn.sigmoid(x)


def _bm_rows(width):
    return pl.BlockSpec((ROW_TILE, width), lambda b, i: (b * SEQ_TILES + i, 0))


def _even_in_kernel(x_ref, g_ref, w_ref, qg_ref, kg_ref, cos_ref, sina_ref, sinb_ref,
                    cdft_ref, xcs_ref, fgate_ref, q_ref, k_ref, v_ref, agate_ref):
    hb = _rms_norm(x_ref[...], g_ref[...]).astype(BF16)

    def proj(lo, width):
        return jnp.dot(hb, w_ref[:, lo:lo + width], preferred_element_type=F32)

    cos = cos_ref[...]
    sina = sina_ref[...]
    sinb = sinb_ref[...]

    def norm_rope(y, gain):
        yn = _rms_norm(y, gain)
        return (yn * cos + pltpu.roll(yn, HEAD_DIM - ROPE_AXIS_DIM // 2, 1) * sina
                + pltpu.roll(yn, ROPE_AXIS_DIM // 2, 1) * sinb)

    q_lo = 2 * FOURIER_WIDTH
    q = proj(q_lo, ATTN_WIDTH)
    for h in range(N_Q_HEADS):
        lo = h * HEAD_DIM
        qh = norm_rope(q[:, lo:lo + HEAD_DIM], qg_ref[...]) * QK_SCALE
        q_ref[:, lo:lo + HEAD_DIM] = qh.astype(BF16)
    agate_ref[...] = _silu(proj(q_lo + ATTN_WIDTH + 2 * KV_WIDTH, ATTN_WIDTH))
    k = proj(q_lo + ATTN_WIDTH, KV_WIDTH)
    for h in range(N_KV_HEADS):
        lo = h * HEAD_DIM
        k_ref[:, lo:lo + HEAD_DIM] = norm_rope(k[:, lo:lo + HEAD_DIM], kg_ref[...]).astype(BF16)
    fgate_ref[...] = _silu(proj(FOURIER_WIDTH, FOURIER_WIDTH))

    f_in = proj(0, FOURIER_WIDTH).astype(BF16)
    for g in range(N_FOURIER_GROUPS):
        lo = g * FOURIER_GROUP
        z = jnp.dot(f_in[:, lo:lo + FOURIER_GROUP], cdft_ref[...], preferred_element_type=F32)
        xcs_ref[0, :, lo:lo + FOURIER_GROUP] = z[:, :FOURIER_GROUP].astype(BF16)
        xcs_ref[1, :, lo:lo + FOURIER_GROUP] = z[:, FOURIER_GROUP:].astype(BF16)
    v_ref[...] = proj(q_lo + ATTN_WIDTH + KV_WIDTH, KV_WIDTH).astype(BF16)


def _even_in(x, norm_g, w_in, q_gain, k_gain, rope, cdft):
    fixed = lambda b, i: (0, 0)
    table = pl.BlockSpec((ROW_TILE, HEAD_DIM), lambda b, i: (i, 0))
    return pl.pallas_call(
        _even_in_kernel,
        grid=(BATCH, SEQ_TILES),
        in_specs=[
            _bm_rows(D_MODEL),
            _resident((1, D_MODEL), fixed),
            _resident((D_MODEL, EVEN_IN_WIDTH), fixed),
            _resident((1, HEAD_DIM), fixed),
            _resident((1, HEAD_DIM), fixed),
            table, table, table,
            _resident((FOURIER_GROUP, 2 * FOURIER_GROUP), fixed),
        ],
        out_specs=[
            pl.BlockSpec((2, ROW_TILE, FOURIER_WIDTH), lambda b, i: (0, b * SEQ_TILES + i, 0)),
            _bm_rows(FOURIER_WIDTH),
            _bm_rows(ATTN_WIDTH),
            _bm_rows(KV_WIDTH),
            _bm_rows(KV_WIDTH),
            _bm_rows(ATTN_WIDTH),
        ],
        out_shape=[
            jax.ShapeDtypeStruct((2, TOKENS, FOURIER_WIDTH), BF16),
            jax.ShapeDtypeStruct((TOKENS, FOURIER_WIDTH), F32),
            jax.ShapeDtypeStruct((TOKENS, ATTN_WIDTH), BF16),
            jax.ShapeDtypeStruct((TOKENS, KV_WIDTH), BF16),
            jax.ShapeDtypeStruct((TOKENS, KV_WIDTH), BF16),
            jax.ShapeDtypeStruct((TOKENS, ATTN_WIDTH), F32),
        ],
        compiler_params=_params(2),
        name="even_in_proj",
    )(x, norm_g, w_in, q_gain, k_gain, *rope, cdft)


def _fourier_kernel(xcs_ref, cos_ref, sin_ref, alt_ref, flip_ref, fw_ref, gate_ref, o_ref):
    half = SEQ // 2
    scale = np.float32(1.0 / np.sqrt(SEQ * FOURIER_GROUP))
    xc = xcs_ref[0]
    p = jnp.dot(cos_ref[...], xc, preferred_element_type=F32)
    q = jnp.dot(sin_ref[...], xcs_ref[1], preferred_element_type=F32)
    nyquist = jnp.dot(alt_ref[...], xc, preferred_element_type=F32)[0:1, :]
    y_lo = ((p - q) * scale).astype(BF16)
    mirrored = jnp.dot(flip_ref[...], ((p + q) * scale).astype(BF16), preferred_element_type=F32)
    row = lax.broadcasted_iota(jnp.int32, (half, FOURIER_GROUP), 0)
    y_hi = jnp.where(row == 0, nyquist * scale, mirrored).astype(BF16)
    fw = fw_ref[0]
    o_ref[:half, :] = (jnp.dot(y_lo, fw, preferred_element_type=F32) * gate_ref[:half, :]).astype(BF16)
    o_ref[half:, :] = (jnp.dot(y_hi, fw, preferred_element_type=F32) * gate_ref[half:, :]).astype(BF16)


def _fourier_mix(xcs, seq_dft, fourier_w, fgate):
    blk = lambda b, g: (b, g)
    fixed = lambda b, g: (0, 0)
    return pl.pallas_call(
        _fourier_kernel,
        grid=(BATCH, N_FOURIER_GROUPS),
        in_specs=[
            pl.BlockSpec((2, SEQ, FOURIER_GROUP), lambda b, g: (0, b, g)),
            _resident((SEQ // 2, SEQ), fixed),
            _resident((SEQ // 2, SEQ), fixed),
            _resident((8, SEQ), fixed),
            _resident((SEQ // 2, SEQ // 2), fixed),
            pl.BlockSpec((1, FOURIER_GROUP, FOURIER_GROUP), lambda b, g: (g, 0, 0)),
            pl.BlockSpec((SEQ, FOURIER_GROUP), blk),
        ],
        out_specs=pl.BlockSpec((SEQ, FOURIER_GROUP), blk),
        out_shape=jax.ShapeDtypeStruct((TOKENS, FOURIER_WIDTH), BF16),
        compiler_params=_params(2),
        name="fourier_mix",
    )(xcs, *seq_dft, fourier_w, fgate)


def _attention_kernel(q_ref, k_ref, v_ref, gate_ref, o_ref, vaug_ref):
    vaug_ref[:, :HEAD_DIM] = v_ref[...]
    vaug_ref[:, HEAD_DIM:] = jnp.ones((SEQ, HEAD_DIM), BF16)
    k = k_ref[...]

    items = [(slice(i * ATTN_Q_TILE, (i + 1) * ATTN_Q_TILE), slice(h * HEAD_DIM, (h + 1) * HEAD_DIM))
             for i in range(SEQ // ATTN_Q_TILE) for h in range(Q_PER_KV)]

    def scores(item):
        return lax.dot_general(q_ref[item], k, (((1,), (1,)), ((), ())),
                               preferred_element_type=F32)

    s_next = scores(items[0])
    for n, item in enumerate(items):
        s = s_next
        if n + 1 < len(items):
            s_next = scores(items[n + 1])
        p = jnp.exp2(s - jnp.max(s, axis=-1, keepdims=True)).astype(BF16)
        ol = jnp.dot(p, vaug_ref[...], preferred_element_type=F32)
        o = ol[:, :HEAD_DIM] / ol[:, HEAD_DIM:]
        o_ref[item] = (o * gate_ref[item]).astype(BF16)


def _attention(q, k, v, agate):
    group_width = Q_PER_KV * HEAD_DIM
    blk = lambda b, h: (b, h)
    return pl.pallas_call(
        _attention_kernel,
        grid=(BATCH, N_KV_HEADS),
        in_specs=[
            pl.BlockSpec((SEQ, group_width), blk),
            pl.BlockSpec((SEQ, HEAD_DIM), blk),
            pl.BlockSpec((SEQ, HEAD_DIM), blk),
            pl.BlockSpec((SEQ, group_width), blk),
        ],
        out_specs=pl.BlockSpec((SEQ, group_width), blk),
        out_shape=jax.ShapeDtypeStruct((TOKENS, ATTN_WIDTH), BF16),
        scratch_shapes=[pltpu.VMEM((SEQ, 2 * HEAD_DIM), BF16)],
        compiler_params=_params(2),
        name="gqa_attention",
    )(q, k, v, agate)


def _out_proj_kernel(*refs, n_mix, final_norm):
    x_ref = refs[0]
    mix_refs = refs[1:1 + n_mix]
    w_ref = refs[-1]
    o_ref = refs[-2]

    @pl.when(pl.program_id(0) == 0)
    def _():
        w_ref[...] = refs[1 + n_mix][...].astype(BF16)

    acc = x_ref[...]
    lo = 0
    for m_ref in mix_refs:
        width = m_ref.shape[1]
        acc = acc + jnp.dot(m_ref[...], w_ref[lo:lo + width, :], preferred_element_type=F32)
        lo += width
    if final_norm:
        acc = _rms_norm(acc, refs[2 + n_mix][...])
    o_ref[...] = acc


def _out_proj(x, mixes, w_out, layer, final_g=None):
    row = lambda i: (i, 0)
    fixed = lambda i: (0, 0)
    in_specs = [pl.BlockSpec((OUT_ROW_TILE, D_MODEL), row)]
    in_specs += [pl.BlockSpec((OUT_ROW_TILE, m.shape[1]), row) for m in mixes]
    in_specs += [_resident((MIX_WIDTH, D_MODEL), lambda i: (layer, 0))]
    args = [x, *mixes, w_out.reshape(-1, D_MODEL)]
    if final_g is not None:
        in_specs += [_resident((1, D_MODEL), fixed)]
        args += [final_g]
    return pl.pallas_call(
        functools.partial(_out_proj_kernel, n_mix=len(mixes), final_norm=final_g is not None),
        grid=(TOKENS // OUT_ROW_TILE,),
        in_specs=in_specs,
        out_specs=pl.BlockSpec((OUT_ROW_TILE, D_MODEL), row),
        out_shape=jax.ShapeDtypeStruct((TOKENS, D_MODEL), F32),
        scratch_shapes=[pltpu.VMEM((MIX_WIDTH, D_MODEL), BF16)],
        compiler_params=_params(1),
        name="out_proj",
    )(*args)


def _odd_in_kernel(x_ref, g_ref, w32_ref, xr_ref, gate_ref, w_ref):
    @pl.when(pl.program_id(0) == 0)
    def _():
        w_ref[...] = w32_ref[...].astype(BF16)

    hb = _rms_norm(x_ref[...], g_ref[...]).astype(BF16)
    gate_ref[...] = _silu(jnp.dot(hb, w_ref[:, RNN_WIDTH:], preferred_element_type=F32))
    xr_ref[...] = jnp.dot(hb, w_ref[:, :RNN_WIDTH], preferred_element_type=F32)


def _odd_in(x, norm_g, w_in, layer):
    row = lambda i: (i, 0)
    fixed = lambda i: (0, 0)
    return pl.pallas_call(
        _odd_in_kernel,
        grid=(TOKENS // ROW_TILE,),
        in_specs=[
            pl.BlockSpec((ROW_TILE, D_MODEL), row),
            _resident((1, D_MODEL), fixed),
            _resident((D_MODEL, 2 * RNN_WIDTH), lambda i: (layer, 0)),
        ],
        out_specs=[pl.BlockSpec((ROW_TILE, RNN_WIDTH), row)] * 2,
        out_shape=[jax.ShapeDtypeStruct((TOKENS, RNN_WIDTH), F32)] * 2,
        scratch_shapes=[pltpu.VMEM((D_MODEL, 2 * RNN_WIDTH), BF16)],
        compiler_params=_params(1),
        name="odd_in_proj",
    )(x, norm_g, w_in.reshape(-1, 2 * RNN_WIDTH))


def _rglru_kernel(xr_ref, prev_ref, next_ref, cw_ref, cb_ref, gw_ref, gb_ref, lam_ref,
                  gate_ref, y_ref, ext_ref, xh_all_ref, hf_all_ref, h_ref):
    phase = pl.program_id(1)
    j = pl.program_id(2)
    n_rows = SCAN_CHUNK_ROWS

    @pl.when(j == 0)
    def _():
        h_ref[...] = jnp.zeros_like(h_ref)

    def chunk_scan(base, reverse, emit):
        half_log_decay = (-0.5 * RG_LRU_C) * jax.nn.softplus(-lam_ref[0])
        half_bias = 0.5 * gb_ref[0]
        order = (lambda r: reversed(r)) if reverse else (lambda r: r)
        h = h_ref[...]
        for lo in order(range(0, n_rows, GATE_ROWS)):
            xh = xh_all_ref[pl.ds(base + lo, GATE_ROWS), :]
            zh = jnp.dot(xh.astype(BF16), gw_ref[0, 0], preferred_element_type=F32) + half_bias
            t_r = jnp.tanh(zh[:, :RNN_BLOCK])
            t_i = jnp.tanh(zh[:, RNN_BLOCK:])
            log_a = half_log_decay * t_r + half_log_decay
            a = jnp.exp(log_a)
            w = jnp.tanh(log_a) * (-1.0 - a * a)
            sqrt_w = w * lax.rsqrt(jnp.maximum(w, F32_MIN_NORMAL))
            u = sqrt_w * (xh * t_i + xh)
            for r in order(range(0, GATE_ROWS, BATCH)):
                h = a[r:r + BATCH, :] * h + u[r:r + BATCH, :]
                emit(lo + r, h)
        h_ref[...] = h

    @pl.when(phase == 0)
    def _():
        base = pl.multiple_of(j * n_rows, n_rows)
        ext_ref[:HALO_PREV_ROWS, :] = jnp.where(j > 0, prev_ref[...], 0.0)
        ext_ref[HALO_PREV_ROWS:HALO_PREV_ROWS + n_rows, :] = xr_ref[...]
        ext_ref[HALO_PREV_ROWS + n_rows:, :] = jnp.where(j < N_SCAN_CHUNKS - 1, next_ref[...], 0.0)
        half_w = 0.5 * cw_ref[...]
        xh = 0.5 * cb_ref[...] + ext_ref[0:n_rows, :] * half_w[0:1, :]
        for tap in range(1, CONV_WIDTH):
            xh = xh + ext_ref[tap * BATCH:tap * BATCH + n_rows, :] * half_w[tap:tap + 1, :]
        xh_all_ref[pl.ds(base, n_rows), :] = xh

        def emit(r, h):
            hf_all_ref[pl.ds(base + r, BATCH), :] = h

        chunk_scan(base, False, emit)

    @pl.when(phase == 1)
    def _():
        base = pl.multiple_of((N_SCAN_CHUNKS - 1 - j) * n_rows, n_rows)

        def emit(r, h):
            y = (hf_all_ref[pl.ds(base + r, BATCH), :] + h) * gate_ref[r:r + BATCH, :]
            y_ref[r:r + BATCH, :] = y.astype(BF16)

        chunk_scan(base, True, emit)


def _rglru_mix(xr, gate, conv_w, conv_b, gate_w, gate_b, lam):
    n = N_SCAN_CHUNKS
    rows = SCAN_CHUNK_ROWS
    prev_per_chunk = rows // HALO_PREV_ROWS
    next_per_chunk = rows // HALO_NEXT_ROWS
    last_next = TOKENS // HALO_NEXT_ROWS - 1

    def fwd_chunk(p, j):
        return j * (1 - p) + (n - 1) * p

    def bwd_chunk(p, j):
        return (n - 1) - j * p

    return pl.pallas_call(
        _rglru_kernel,
        grid=(N_RNN_BLOCKS, 2, n),
        in_specs=[
            pl.BlockSpec((rows, RNN_BLOCK), lambda c, p, j: (fwd_chunk(p, j), c)),
            pl.BlockSpec((HALO_PREV_ROWS, RNN_BLOCK),
                         lambda c, p, j: (jnp.maximum(fwd_chunk(p, j) * prev_per_chunk - 1, 0), c)),
            pl.BlockSpec((HALO_NEXT_ROWS, RNN_BLOCK),
                         lambda c, p, j: (jnp.minimum((fwd_chunk(p, j) + 1) * next_per_chunk, last_next), c)),
            pl.BlockSpec((CONV_WIDTH, RNN_BLOCK), lambda c, p, j: (0, c)),
            pl.BlockSpec((1, RNN_BLOCK), lambda c, p, j: (0, c)),
            pl.BlockSpec((1, 1, RNN_BLOCK, 2 * RNN_BLOCK), lambda c, p, j: (p, c, 0, 0)),
            pl.BlockSpec((1, 1, 2 * RNN_BLOCK), lambda c, p, j: (p * N_RNN_BLOCKS + c, 0, 0)),
            pl.BlockSpec((1, 1, RNN_BLOCK), lambda c, p, j: (p * N_RNN_BLOCKS + c, 0, 0)),
            pl.BlockSpec((rows, RNN_BLOCK), lambda c, p, j: (bwd_chunk(p, j), c)),
        ],
        out_specs=pl.BlockSpec((rows, RNN_BLOCK), lambda c, p, j: (bwd_chunk(p, j), c)),
        out_shape=jax.ShapeDtypeStruct((TOKENS, RNN_WIDTH), BF16),
        scratch_shapes=[
            pltpu.VMEM((rows + HALO_PREV_ROWS + HALO_NEXT_ROWS, RNN_BLOCK), F32),
            pltpu.VMEM((TOKENS, RNN_BLOCK), F32),
            pltpu.VMEM((TOKENS, RNN_BLOCK), F32),
            pltpu.VMEM((BATCH, RNN_BLOCK), F32),
        ],
        compiler_params=_params(3),
        name="rglru_mix",
    )(xr, xr, xr, conv_w, conv_b, gate_w, gate_b, lam, gate)


def _dft_angles(n_out, n):
    j = np.arange(n)
    return 2.0 * np.pi * ((np.arange(n_out)[:, None] * j[None, :]) % n) / n


def _as_operand(table):
    return jnp.asarray(table.astype(np.float32)).astype(BF16)


def _channel_dft():
    ang = _dft_angles(FOURIER_GROUP, FOURIER_GROUP)
    return _as_operand(np.concatenate([np.cos(ang), np.sin(ang)], axis=1))


def _sequence_dft():
    half = SEQ // 2
    ang = _dft_angles(half, SEQ)
    alternating = np.zeros((8, SEQ))
    alternating[0] = 1.0 - 2.0 * (np.arange(SEQ) % 2)
    flip = np.zeros((half, half))
    i = np.arange(1, half)
    flip[i, half - i] = 1.0
    return tuple(_as_operand(t) for t in (np.cos(ang), np.sin(ang), alternating, flip))


def _rope_tables():
    rows = SEQ // GRID_W
    row = jnp.broadcast_to(jnp.arange(rows)[:, None], (rows, GRID_W)).reshape(-1).astype(F32)
    col = jnp.broadcast_to(jnp.arange(GRID_W)[None, :], (rows, GRID_W)).reshape(-1).astype(F32)
    inv_freq = ROPE_THETA ** (-jnp.arange(0, ROPE_AXIS_DIM, 2, dtype=F32) / ROPE_AXIS_DIM)
    ang_r = row[:, None] * inv_freq[None, :]
    ang_c = col[:, None] * inv_freq[None, :]
    zero = jnp.zeros_like(ang_r)
    cos = jnp.concatenate([jnp.cos(ang_r)] * 2 + [jnp.cos(ang_c)] * 2, axis=-1)
    sin_first = jnp.concatenate([-jnp.sin(ang_r), zero, -jnp.sin(ang_c), zero], axis=-1)
    sin_second = jnp.concatenate([zero, jnp.sin(ang_r), zero, jnp.sin(ang_c)], axis=-1)
    return cos, sin_first, sin_second


def _even_layer(x, norm_g, w_in, fourier_w, q_gain, k_gain, w_out, tables):
    rope, cdft, seq_dft = tables
    xcs, fgate, q, k, v, agate = _even_in(
        x, norm_g.reshape(1, D_MODEL), w_in.astype(BF16),
        q_gain.reshape(1, HEAD_DIM), k_gain.reshape(1, HEAD_DIM), rope, cdft)
    f_out = _fourier_mix(xcs, seq_dft, fourier_w.astype(BF16), fgate)
    a_out = _attention(q, k, v, agate)
    return _out_proj(x, [f_out, a_out], *w_out)


def _odd_layer(x, norm_g, w_in, conv_w, conv_b, w_a, b_a, w_x, b_x, lam, w_out, final_g):
    xr, gate = _odd_in(x, norm_g.reshape(1, D_MODEL), *w_in)
    gate_w = jnp.concatenate([w_a, w_x], axis=-1).astype(BF16)
    gate_b = jnp.concatenate([b_a, b_x], axis=-1).reshape(2 * N_RNN_BLOCKS, 1, 2 * RNN_BLOCK)
    y = _rglru_mix(xr, gate, conv_w, conv_b.reshape(1, RNN_WIDTH), gate_w, gate_b,
                   lam.reshape(2 * N_RNN_BLOCKS, 1, RNN_BLOCK))
    return _out_proj(x, [y], *w_out, final_g)


def kernel(x, even_norm, even_w_in, fourier_w, q_gain, k_gain, even_w_out, odd_norm, odd_w_in,
           conv_w, conv_b, gate_a_w, gate_a_b, gate_x_w, gate_x_b, rglru_lambda, odd_w_out,
           final_norm):
    tables = (_rope_tables(), _channel_dft(), _sequence_dft())
    def reorder(a, major, minor):
        return jnp.transpose(a.reshape(major, minor, D_MODEL), (1, 0, 2)).reshape(TOKENS, D_MODEL)

    h = x.reshape(TOKENS, D_MODEL)
    for layer in range(DEPTH):
        i = layer // 2
        if layer % 2 == 0:
            if layer > 0:
                h = reorder(h, SEQ, BATCH)
            h = _even_layer(h, even_norm[i], even_w_in[i], fourier_w[i], q_gain[i],
                            k_gain[i], (even_w_out, i), tables)
        else:
            final_g = final_norm.reshape(1, D_MODEL) if layer == DEPTH - 1 else None
            h = _odd_layer(reorder(h, BATCH, SEQ), odd_norm[i], (odd_w_in, i), conv_w[i], conv_b[i],
                           gate_a_w[i], gate_a_b[i], gate_x_w[i], gate_x_b[i], rglru_lambda[i],
                           (odd_w_out, i), final_g)
    return reorder(h, SEQ, BATCH).reshape(BATCH, SEQ, D_MODEL)
```

```python
import functools

import jax
import jax.numpy as jnp
import numpy as np
from jax import lax
from jax.experimental import pallas as pl
from jax.experimental.pallas import tpu as pltpu

D_MODEL = 1024
BATCH = 16
SEQ = 2048
DEPTH = 4
TOKENS = BATCH * SEQ

GRID_W = 64
MIX_WIDTH = 2 * D_MODEL
FOURIER_WIDTH = MIX_WIDTH // 2
N_FOURIER_GROUPS = 4
FOURIER_GROUP = FOURIER_WIDTH // N_FOURIER_GROUPS
ATTN_WIDTH = MIX_WIDTH - FOURIER_WIDTH
HEAD_DIM = 128
N_Q_HEADS = ATTN_WIDTH // HEAD_DIM
N_KV_HEADS = 2
Q_PER_KV = N_Q_HEADS // N_KV_HEADS
KV_WIDTH = N_KV_HEADS * HEAD_DIM
ROPE_THETA = 10000.0
ROPE_AXIS_DIM = HEAD_DIM // 2
RNN_WIDTH = MIX_WIDTH
N_RNN_BLOCKS = 16
RNN_BLOCK = RNN_WIDTH // N_RNN_BLOCKS
CONV_WIDTH = 4
CONV_LEFT = 2
RG_LRU_C = 8.0
NORM_EPS = 1e-6
EVEN_IN_WIDTH = 2 * FOURIER_WIDTH + ATTN_WIDTH + 2 * KV_WIDTH + ATTN_WIDTH

F32 = jnp.float32
BF16 = jnp.bfloat16
F32_MIN_NORMAL = float(np.finfo(np.float32).tiny)

VMEM_LIMIT_BYTES = 56 * 1024 * 1024

ROW_TILE = 512
SEQ_TILES = SEQ // ROW_TILE
OUT_ROW_TILE = 1024
FOURIER_GROUPS_PER_STEP = 2
FOURIER_STEP_WIDTH = FOURIER_GROUPS_PER_STEP * FOURIER_GROUP
ATTN_Q_TILE = 512
SCAN_CHUNK_STEPS = 256
SCAN_CHUNK_ROWS = SCAN_CHUNK_STEPS * BATCH
N_SCAN_CHUNKS = SEQ // SCAN_CHUNK_STEPS
GATE_ROWS = 512
HALO_PREV_ROWS = CONV_LEFT * BATCH
HALO_NEXT_ROWS = (CONV_WIDTH - 1 - CONV_LEFT) * BATCH

QK_SCALE = float(HEAD_DIM ** -0.5 * np.log2(np.e))


def _resident(block_shape, index_map):
    return pl.BlockSpec(block_shape, index_map, pipeline_mode=pl.Buffered(1))


def _params(n_axes):
    return pltpu.CompilerParams(
        dimension_semantics=("arbitrary",) * n_axes,
        vmem_limit_bytes=VMEM_LIMIT_BYTES)


def _rms_norm(x, gain):
    return x * lax.rsqrt(jnp.mean(x * x, axis=-1, keepdims=True) + NORM_EPS) * gain


def _silu(x):
    return x * jax.nn.sigmoid(x)


def _bm_rows(width):
    return pl.BlockSpec((ROW_TILE, width), lambda b, i: (b * SEQ_TILES + i, 0))


def _even_in_kernel(x_ref, g_ref, w_ref, qg_ref, kg_ref, cos_ref, sina_ref, sinb_ref,
                    cdft_ref, xcs_ref, fgate_ref, q_ref, k_ref, v_ref, agate_ref):
    hb = _rms_norm(x_ref[...], g_ref[...]).astype(BF16)

    def proj(lo, width):
        return jnp.dot(hb, w_ref[:, lo:lo + width], preferred_element_type=F32)

    cos = cos_ref[...]
    sina = sina_ref[...]
    sinb = sinb_ref[...]

    def norm_rope(y, gain):
        yn = _rms_norm(y, gain)
        return (yn * cos + pltpu.roll(yn, HEAD_DIM - ROPE_AXIS_DIM // 2, 1) * sina
                + pltpu.roll(yn, ROPE_AXIS_DIM // 2, 1) * sinb)

    q_lo = 2 * FOURIER_WIDTH
    q = proj(q_lo, ATTN_WIDTH)
    for h in range(N_Q_HEADS):
        lo = h * HEAD_DIM
        qh = norm_rope(q[:, lo:lo + HEAD_DIM], qg_ref[...]) * QK_SCALE
        q_ref[:, lo:lo + HEAD_DIM] = qh.astype(BF16)
    agate_ref[...] = _silu(proj(q_lo + ATTN_WIDTH + 2 * KV_WIDTH, ATTN_WIDTH))
    k = proj(q_lo + ATTN_WIDTH, KV_WIDTH)
    for h in range(N_KV_HEADS):
        lo = h * HEAD_DIM
        k_ref[:, lo:lo + HEAD_DIM] = norm_rope(k[:, lo:lo + HEAD_DIM], kg_ref[...]).astype(BF16)
    fgate_ref[...] = _silu(proj(FOURIER_WIDTH, FOURIER_WIDTH))

    f_in = proj(0, FOURIER_WIDTH).astype(BF16)
    for g in range(N_FOURIER_GROUPS):
        lo = g * FOURIER_GROUP
        z = jnp.dot(f_in[:, lo:lo + FOURIER_GROUP], cdft_ref[...], preferred_element_type=F32)
        xcs_ref[0, :, lo:lo + FOURIER_GROUP] = z[:, :FOURIER_GROUP].astype(BF16)
        xcs_ref[1, :, lo:lo + FOURIER_GROUP] = z[:, FOURIER_GROUP:].astype(BF16)
    v_ref[...] = proj(q_lo + ATTN_WIDTH + KV_WIDTH, KV_WIDTH).astype(BF16)


def _even_in(x, norm_g, w_in, q_gain, k_gain, rope, cdft):
    fixed = lambda b, i: (0, 0)
    table = pl.BlockSpec((ROW_TILE, HEAD_DIM), lambda b, i: (i, 0))
    return pl.pallas_call(
        _even_in_kernel,
        grid=(BATCH, SEQ_TILES),
        in_specs=[
            _bm_rows(D_MODEL),
            _resident((1, D_MODEL), fixed),
            _resident((D_MODEL, EVEN_IN_WIDTH), fixed),
            _resident((1, HEAD_DIM), fixed),
            _resident((1, HEAD_DIM), fixed),
            table, table, table,
            _resident((FOURIER_GROUP, 2 * FOURIER_GROUP), fixed),
        ],
        out_specs=[
            pl.BlockSpec((2, ROW_TILE, FOURIER_WIDTH), lambda b, i: (0, b * SEQ_TILES + i, 0)),
            _bm_rows(FOURIER_WIDTH),
            _bm_rows(ATTN_WIDTH),
            _bm_rows(KV_WIDTH),
            _bm_rows(KV_WIDTH),
            _bm_rows(ATTN_WIDTH),
        ],
        out_shape=[
            jax.ShapeDtypeStruct((2, TOKENS, FOURIER_WIDTH), BF16),
            jax.ShapeDtypeStruct((TOKENS, FOURIER_WIDTH), F32),
            jax.ShapeDtypeStruct((TOKENS, ATTN_WIDTH), BF16),
            jax.ShapeDtypeStruct((TOKENS, KV_WIDTH), BF16),
            jax.ShapeDtypeStruct((TOKENS, KV_WIDTH), BF16),
            jax.ShapeDtypeStruct((TOKENS, ATTN_WIDTH), F32),
        ],
        compiler_params=_params(2),
        name="even_in_proj",
    )(x, norm_g, w_in, q_gain, k_gain, *rope, cdft)


def _fourier_kernel(xcs_ref, cos_ref, sin_ref, alt_ref, flip_ref, fw_ref, gate_ref, o_ref):
    half = SEQ // 2
    scale = np.float32(1.0 / np.sqrt(SEQ * FOURIER_GROUP))
    xc = xcs_ref[0]
    p = jnp.dot(cos_ref[...], xc, preferred_element_type=F32)
    q = jnp.dot(sin_ref[...], xcs_ref[1], preferred_element_type=F32)
    nyquist = jnp.dot(alt_ref[...], xc, preferred_element_type=F32)[0:1, :]
    y_lo = ((p - q) * scale).astype(BF16)
    mirrored = jnp.dot(flip_ref[...], ((p + q) * scale).astype(BF16), preferred_element_type=F32)
    row = lax.broadcasted_iota(jnp.int32, (half, FOURIER_STEP_WIDTH), 0)
    y_hi = jnp.where(row == 0, nyquist * scale, mirrored).astype(BF16)
    for g in range(FOURIER_GROUPS_PER_STEP):
        cols = slice(g * FOURIER_GROUP, (g + 1) * FOURIER_GROUP)
        fw = fw_ref[g]
        o_ref[:half, cols] = (jnp.dot(y_lo[:, cols], fw, preferred_element_type=F32)
                              * gate_ref[:half, cols]).astype(BF16)
        o_ref[half:, cols] = (jnp.dot(y_hi[:, cols], fw, preferred_element_type=F32)
                              * gate_ref[half:, cols]).astype(BF16)


def _fourier_mix(xcs, seq_dft, fourier_w, fgate):
    blk = lambda b, g: (b, g)
    fixed = lambda b, g: (0, 0)
    return pl.pallas_call(
        _fourier_kernel,
        grid=(BATCH, N_FOURIER_GROUPS // FOURIER_GROUPS_PER_STEP),
        in_specs=[
            pl.BlockSpec((2, SEQ, FOURIER_STEP_WIDTH), lambda b, g: (0, b, g)),
            _resident((SEQ // 2, SEQ), fixed),
            _resident((SEQ // 2, SEQ), fixed),
            _resident((8, SEQ), fixed),
            _resident((SEQ // 2, SEQ // 2), fixed),
            pl.BlockSpec((FOURIER_GROUPS_PER_STEP, FOURIER_GROUP, FOURIER_GROUP),
                         lambda b, g: (g, 0, 0)),
            pl.BlockSpec((SEQ, FOURIER_STEP_WIDTH), blk),
        ],
        out_specs=pl.BlockSpec((SEQ, FOURIER_STEP_WIDTH), blk),
        out_shape=jax.ShapeDtypeStruct((TOKENS, FOURIER_WIDTH), BF16),
        compiler_params=_params(2),
        name="fourier_mix",
    )(xcs, *seq_dft, fourier_w, fgate)


def _attention_kernel(q_ref, k_ref, v_ref, gate_ref, o_ref, vaug_ref):
    vaug_ref[:, :HEAD_DIM] = v_ref[...]
    vaug_ref[:, HEAD_DIM:] = jnp.ones((SEQ, HEAD_DIM), BF16)
    k = k_ref[...]

    items = [(slice(i * ATTN_Q_TILE, (i + 1) * ATTN_Q_TILE), slice(h * HEAD_DIM, (h + 1) * HEAD_DIM))
             for i in range(SEQ // ATTN_Q_TILE) for h in range(Q_PER_KV)]

    def scores(item):
        return lax.dot_general(q_ref[item], k, (((1,), (1,)), ((), ())),
                               preferred_element_type=F32)

    s_next = scores(items[0])
    for n, item in enumerate(items):
        s = s_next
        if n + 1 < len(items):
            s_next = scores(items[n + 1])
        p = jnp.exp2(s - jnp.max(s, axis=-1, keepdims=True)).astype(BF16)
        ol = jnp.dot(p, vaug_ref[...], preferred_element_type=F32)
        o = ol[:, :HEAD_DIM] / ol[:, HEAD_DIM:]
        o_ref[item] = (o * gate_ref[item]).astype(BF16)


def _attention(q, k, v, agate):
    group_width = Q_PER_KV * HEAD_DIM
    blk = lambda b, h: (b, h)
    return pl.pallas_call(
        _attention_kernel,
        grid=(BATCH, N_KV_HEADS),
        in_specs=[
            pl.BlockSpec((SEQ, group_width), blk),
            pl.BlockSpec((SEQ, HEAD_DIM), blk),
            pl.BlockSpec((SEQ, HEAD_DIM), blk),
            pl.BlockSpec((SEQ, group_width), blk),
        ],
        out_specs=pl.BlockSpec((SEQ, group_width), blk),
        out_shape=jax.ShapeDtypeStruct((TOKENS, ATTN_WIDTH), BF16),
        scratch_shapes=[pltpu.VMEM((SEQ, 2 * HEAD_DIM), BF16)],
        compiler_params=_params(2),
        name="gqa_attention",
    )(q, k, v, agate)


def _out_proj_kernel(*refs, n_mix, final_norm):
    x_ref = refs[0]
    mix_refs = refs[1:1 + n_mix]
    w_ref = refs[-1]
    o_ref = refs[-2]

    @pl.when(pl.program_id(0) == 0)
    def _():
        w_ref[...] = refs[1 + n_mix][...].astype(BF16)

    acc = x_ref[...]
    lo = 0
    for m_ref in mix_refs:
        width = m_ref.shape[1]
        acc = acc + jnp.dot(m_ref[...], w_ref[lo:lo + width, :], preferred_element_type=F32)
        lo += width
    if final_norm:
        acc = _rms_norm(acc, refs[2 + n_mix][...])
    o_ref[...] = acc


def _out_proj(x, mixes, w_out, layer, final_g=None):
    row = lambda i: (i, 0)
    fixed = lambda i: (0, 0)
    in_specs = [pl.BlockSpec((OUT_ROW_TILE, D_MODEL), row)]
    in_specs += [pl.BlockSpec((OUT_ROW_TILE, m.shape[1]), row) for m in mixes]
    in_specs += [_resident((None, MIX_WIDTH, D_MODEL), lambda i: (layer, 0, 0))]
    args = [x, *mixes, w_out]
    if final_g is not None:
        in_specs += [_resident((1, D_MODEL), fixed)]
        args += [final_g]
    return pl.pallas_call(
        functools.partial(_out_proj_kernel, n_mix=len(mixes), final_norm=final_g is not None),
        grid=(TOKENS // OUT_ROW_TILE,),
        in_specs=in_specs,
        out_specs=pl.BlockSpec((OUT_ROW_TILE, D_MODEL), row),
        out_shape=jax.ShapeDtypeStruct((TOKENS, D_MODEL), F32),
        scratch_shapes=[pltpu.VMEM((MIX_WIDTH, D_MODEL), BF16)],
        compiler_params=_params(1),
        name="out_proj",
    )(*args)


def _odd_in_kernel(x_ref, g_ref, w32_ref, xr_ref, gate_ref, w_ref):
    @pl.when(pl.program_id(0) == 0)
    def _():
        w_ref[...] = w32_ref[...].astype(BF16)

    hb = _rms_norm(x_ref[...], g_ref[...]).astype(BF16)
    gate_ref[...] = _silu(jnp.dot(hb, w_ref[:, RNN_WIDTH:], preferred_element_type=F32))
    xr_ref[...] = jnp.dot(hb, w_ref[:, :RNN_WIDTH], preferred_element_type=F32)


def _odd_in(x, norm_g, w_in, layer):
    row = lambda i: (i, 0)
    fixed = lambda i: (0, 0)
    return pl.pallas_call(
        _odd_in_kernel,
        grid=(TOKENS // ROW_TILE,),
        in_specs=[
            pl.BlockSpec((ROW_TILE, D_MODEL), row),
            _resident((1, D_MODEL), fixed),
            _resident((None, D_MODEL, 2 * RNN_WIDTH), lambda i: (layer, 0, 0)),
        ],
        out_specs=[pl.BlockSpec((ROW_TILE, RNN_WIDTH), row)] * 2,
        out_shape=[jax.ShapeDtypeStruct((TOKENS, RNN_WIDTH), F32)] * 2,
        scratch_shapes=[pltpu.VMEM((D_MODEL, 2 * RNN_WIDTH), BF16)],
        compiler_params=_params(1),
        name="odd_in_proj",
    )(x, norm_g, w_in)


def _rglru_kernel(xr_ref, prev_ref, next_ref, cw_ref, cb_ref, gw_ref, gb_ref, lam_ref,
                  gate_ref, y_ref, ext_ref, xh_all_ref, hf_all_ref, h_ref):
    phase = pl.program_id(1)
    j = pl.program_id(2)
    n_rows = SCAN_CHUNK_ROWS

    @pl.when(j == 0)
    def _():
        h_ref[...] = jnp.zeros_like(h_ref)

    def chunk_scan(base, reverse, emit):
        half_log_decay = (-0.5 * RG_LRU_C) * jax.nn.softplus(-lam_ref[0])
        half_bias = 0.5 * gb_ref[0]
        order = (lambda r: reversed(r)) if reverse else (lambda r: r)
        h = h_ref[...]
        for lo in order(range(0, n_rows, GATE_ROWS)):
            xh = xh_all_ref[pl.ds(base + lo, GATE_ROWS), :]
            zh = jnp.dot(xh.astype(BF16), gw_ref[0, 0], preferred_element_type=F32) + half_bias
            t_r = jnp.tanh(zh[:, :RNN_BLOCK])
            t_i = jnp.tanh(zh[:, RNN_BLOCK:])
            log_a = half_log_decay * t_r + half_log_decay
            a = jnp.exp(log_a)
            w = jnp.tanh(log_a) * (-1.0 - a * a)
            sqrt_w = w * lax.rsqrt(jnp.maximum(w, F32_MIN_NORMAL))
            u = sqrt_w * (xh * t_i + xh)
            for r in order(range(0, GATE_ROWS, BATCH)):
                h = a[r:r + BATCH, :] * h + u[r:r + BATCH, :]
                emit(lo + r, h)
        h_ref[...] = h

    @pl.when(phase == 0)
    def _():
        base = pl.multiple_of(j * n_rows, n_rows)
        ext_ref[:HALO_PREV_ROWS, :] = jnp.where(j > 0, prev_ref[...], 0.0)
        ext_ref[HALO_PREV_ROWS:HALO_PREV_ROWS + n_rows, :] = xr_ref[...]
        ext_ref[HALO_PREV_ROWS + n_rows:, :] = jnp.where(j < N_SCAN_CHUNKS - 1, next_ref[...], 0.0)
        half_w = 0.5 * cw_ref[...]
        xh = 0.5 * cb_ref[...] + ext_ref[0:n_rows, :] * half_w[0:1, :]
        for tap in range(1, CONV_WIDTH):
            xh = xh + ext_ref[tap * BATCH:tap * BATCH + n_rows, :] * half_w[tap:tap + 1, :]
        xh_all_ref[pl.ds(base, n_rows), :] = xh

        def emit(r, h):
            hf_all_ref[pl.ds(base + r, BATCH), :] = h

        chunk_scan(base, False, emit)

    @pl.when(phase == 1)
    def _():
        base = pl.multiple_of((N_SCAN_CHUNKS - 1 - j) * n_rows, n_rows)

        def emit(r, h):
            y = (hf_all_ref[pl.ds(base + r, BATCH), :] + h) * gate_ref[r:r + BATCH, :]
            y_ref[r:r + BATCH, :] = y.astype(BF16)

        chunk_scan(base, True, emit)


def _rglru_mix(xr, gate, conv_w, conv_b, gate_w, gate_b, lam):
    n = N_SCAN_CHUNKS
    rows = SCAN_CHUNK_ROWS
    prev_per_chunk = rows // HALO_PREV_ROWS
    next_per_chunk = rows // HALO_NEXT_ROWS
    last_next = TOKENS // HALO_NEXT_ROWS - 1

    def fwd_chunk(p, j):
        return j * (1 - p) + (n - 1) * p

    def bwd_chunk(p, j):
        return (n - 1) - j * p

    return pl.pallas_call(
        _rglru_kernel,
        grid=(N_RNN_BLOCKS, 2, n),
        in_specs=[
            pl.BlockSpec((rows, RNN_BLOCK), lambda c, p, j: (fwd_chunk(p, j), c)),
            pl.BlockSpec((HALO_PREV_ROWS, RNN_BLOCK),
                         lambda c, p, j: (jnp.maximum(fwd_chunk(p, j) * prev_per_chunk - 1, 0), c)),
            pl.BlockSpec((HALO_NEXT_ROWS, RNN_BLOCK),
                         lambda c, p, j: (jnp.minimum((fwd_chunk(p, j) + 1) * next_per_chunk, last_next), c)),
            pl.BlockSpec((CONV_WIDTH, RNN_BLOCK), lambda c, p, j: (0, c)),
            pl.BlockSpec((1, RNN_BLOCK), lambda c, p, j: (0, c)),
            pl.BlockSpec((1, 1, RNN_BLOCK, 2 * RNN_BLOCK), lambda c, p, j: (p, c, 0, 0)),
            pl.BlockSpec((1, 1, 2 * RNN_BLOCK), lambda c, p, j: (p * N_RNN_BLOCKS + c, 0, 0)),
            pl.BlockSpec((1, 1, RNN_BLOCK), lambda c, p, j: (p * N_RNN_BLOCKS + c, 0, 0)),
            pl.BlockSpec((rows, RNN_BLOCK), lambda c, p, j: (bwd_chunk(p, j), c)),
        ],
        out_specs=pl.BlockSpec((rows, RNN_BLOCK), lambda c, p, j: (bwd_chunk(p, j), c)),
        out_shape=jax.ShapeDtypeStruct((TOKENS, RNN_WIDTH), BF16),
        scratch_shapes=[
            pltpu.VMEM((rows + HALO_PREV_ROWS + HALO_NEXT_ROWS, RNN_BLOCK), F32),
            pltpu.VMEM((TOKENS, RNN_BLOCK), F32),
            pltpu.VMEM((TOKENS, RNN_BLOCK), F32),
            pltpu.VMEM((BATCH, RNN_BLOCK), F32),
        ],
        compiler_params=_params(3),
        name="rglru_mix",
    )(xr, xr, xr, conv_w, conv_b, gate_w, gate_b, lam, gate)


def _dft_angles(n_out, n):
    j = np.arange(n)
    return 2.0 * np.pi * ((np.arange(n_out)[:, None] * j[None, :]) % n) / n


def _as_operand(table):
    return jnp.asarray(table.astype(np.float32)).astype(BF16)


def _channel_dft():
    ang = _dft_angles(FOURIER_GROUP, FOURIER_GROUP)
    return _as_operand(np.concatenate([np.cos(ang), np.sin(ang)], axis=1))


def _sequence_dft():
    half = SEQ // 2
    ang = _dft_angles(half, SEQ)
    alternating = np.zeros((8, SEQ))
    alternating[0] = 1.0 - 2.0 * (np.arange(SEQ) % 2)
    flip = np.zeros((half, half))
    i = np.arange(1, half)
    flip[i, half - i] = 1.0
    return tuple(_as_operand(t) for t in (np.cos(ang), np.sin(ang), alternating, flip))


def _rope_tables():
    rows = SEQ // GRID_W
    row = jnp.broadcast_to(jnp.arange(rows)[:, None], (rows, GRID_W)).reshape(-1).astype(F32)
    col = jnp.broadcast_to(jnp.arange(GRID_W)[None, :], (rows, GRID_W)).reshape(-1).astype(F32)
    inv_freq = ROPE_THETA ** (-jnp.arange(0, ROPE_AXIS_DIM, 2, dtype=F32) / ROPE_AXIS_DIM)
    ang_r = row[:, None] * inv_freq[None, :]
    ang_c = col[:, None] * inv_freq[None, :]
    zero = jnp.zeros_like(ang_r)
    cos = jnp.concatenate([jnp.cos(ang_r)] * 2 + [jnp.cos(ang_c)] * 2, axis=-1)
    sin_first = jnp.concatenate([-jnp.sin(ang_r), zero, -jnp.sin(ang_c), zero], axis=-1)
    sin_second = jnp.concatenate([zero, jnp.sin(ang_r), zero, jnp.sin(ang_c)], axis=-1)
    return cos, sin_first, sin_second


def _even_layer(x, norm_g, w_in, fourier_w, q_gain, k_gain, w_out, tables):
    rope, cdft, seq_dft = tables
    xcs, fgate, q, k, v, agate = _even_in(
        x, norm_g.reshape(1, D_MODEL), w_in.astype(BF16),
        q_gain.reshape(1, HEAD_DIM), k_gain.reshape(1, HEAD_DIM), rope, cdft)
    f_out = _fourier_mix(xcs, seq_dft, fourier_w.astype(BF16), fgate)
    a_out = _attention(q, k, v, agate)
    return _out_proj(x, [f_out, a_out], *w_out)


def _odd_layer(x, norm_g, w_in, conv_w, conv_b, w_a, b_a, w_x, b_x, lam, w_out, final_g):
    xr, gate = _odd_in(x, norm_g.reshape(1, D_MODEL), *w_in)
    gate_w = jnp.concatenate([w_a, w_x], axis=-1).astype(BF16)
    gate_b = jnp.concatenate([b_a, b_x], axis=-1).reshape(2 * N_RNN_BLOCKS, 1, 2 * RNN_BLOCK)
    y = _rglru_mix(xr, gate, conv_w, conv_b.reshape(1, RNN_WIDTH), gate_w, gate_b,
                   lam.reshape(2 * N_RNN_BLOCKS, 1, RNN_BLOCK))
    return _out_proj(x, [y], *w_out, final_g)


def kernel(x, even_norm, even_w_in, fourier_w, q_gain, k_gain, even_w_out, odd_norm, odd_w_in,
           conv_w, conv_b, gate_a_w, gate_a_b, gate_x_w, gate_x_b, rglru_lambda, odd_w_out,
           final_norm):
    tables = (_rope_tables(), _channel_dft(), _sequence_dft())
    def reorder(a, major, minor):
        return jnp.transpose(a.reshape(major, minor, D_MODEL), (1, 0, 2)).reshape(TOKENS, D_MODEL)

    h = x.reshape(TOKENS, D_MODEL)
    for layer in range(DEPTH):
        i = layer // 2
        if layer % 2 == 0:
            if layer > 0:
                h = reorder(h, SEQ, BATCH)
            h = _even_layer(h, even_norm[i], even_w_in[i], fourier_w[i], q_gain[i],
                            k_gain[i], (even_w_out, i), tables)
        else:
            final_g = final_norm.reshape(1, D_MODEL) if layer == DEPTH - 1 else None
            h = _odd_layer(reorder(h, BATCH, SEQ), odd_norm[i], (odd_w_in, i), conv_w[i], conv_b[i],
                           gate_a_w[i], gate_a_b[i], gate_x_w[i], gate_x_b[i], rglru_lambda[i],
                           (odd_w_out, i), final_g)
    return reorder(h, SEQ, BATCH).reshape(BATCH, SEQ, D_MODEL)
```
